```python
import math
import jax, jax.numpy as jnp
from jax import lax
import numpy as np

D_MODEL = 1024
BATCH = 8
SEQ = 4096
DEPTH = 2

PL_DIM = 256
N_EVEN = (DEPTH + 1) // 2
N_ODD = DEPTH // 2
CONV_A_WIDTH = D_MODEL
CONV_A_KERNEL = 3
GDN_HEADS = 8
GDN_HEAD_DIM = 128
GDN_WIDTH = GDN_HEADS * GDN_HEAD_DIM
GDN_CONV_KERNEL = 4
GDN_CHUNK = 64
HGRN_HEAD_DIM = 128
HGRN_WIDTH = 2 * D_MODEL
HGRN_HEADS = HGRN_WIDTH // HGRN_HEAD_DIM
HGRN_CHUNK = 32
EVEN_IN = 4 * CONV_A_WIDTH + 4 * GDN_WIDTH + 2 * GDN_HEADS
EVEN_MIX = CONV_A_WIDTH + GDN_WIDTH
ODD_IN = 4 * HGRN_WIDTH
ODD_MIX = HGRN_WIDTH
DEEPNORM_ALPHA = (2.0 * DEPTH) ** 0.25
DEEPNORM_BETA = (8.0 * DEPTH) ** -0.25
NORM_EPS = 1e-5

kernel_name = "hybrid_shortconv_gdn_hgrn2_deepnorm"


def layer_norm(x, g, b):
    xf = x.astype(jnp.float32)
    mu = jnp.mean(xf, axis=-1, keepdims=True)
    var = jnp.mean(jnp.square(xf - mu), axis=-1, keepdims=True)
    return ((xf - mu) * lax.rsqrt(var + NORM_EPS) * g.astype(jnp.float32) + b.astype(jnp.float32)).astype(x.dtype)


def rms_norm(x, g):
    xf = x.astype(jnp.float32)
    y = xf * lax.rsqrt(jnp.mean(jnp.square(xf), axis=-1, keepdims=True) + NORM_EPS)
    return (y * g.astype(jnp.float32)).astype(x.dtype)


def l2_normalize(x):
    xf = x.astype(jnp.float32)
    return (xf * lax.rsqrt(jnp.sum(jnp.square(xf), axis=-1, keepdims=True) + 1e-6)).astype(x.dtype)


def causal_depthwise_conv(x, w):
    k, c = w.shape
    return lax.conv_general_dilated(
        x, w[:, None, :].astype(x.dtype), window_strides=(1,), padding=[(k - 1, 0)],
        dimension_numbers=("NWC", "WIO", "NWC"), feature_group_count=c)


def _to_chunks(t, c):
    b, s, h, d = t.shape
    return t.reshape(b, s // c, c, h, d).transpose(0, 3, 1, 2, 4)


def _from_chunks(t):
    b, h, n, c, d = t.shape
    return t.transpose(0, 2, 3, 1, 4).reshape(b, n * c, h, d)


def gated_delta_rule(q, k, v, g, beta):
    dtype = v.dtype
    c = GDN_CHUNK
    dk, dv = q.shape[-1], v.shape[-1]
    q = _to_chunks(q.astype(jnp.float32) * (dk ** -0.5), c)
    k = _to_chunks(k.astype(jnp.float32), c)
    v = _to_chunks(v.astype(jnp.float32), c)
    g = _to_chunks(g.astype(jnp.float32)[..., None], c)[..., 0]
    beta = _to_chunks(beta.astype(jnp.float32)[..., None], c)
    gc = jnp.cumsum(g, axis=-1)
    incl = jnp.tril(jnp.ones((c, c), bool))
    strict = jnp.tril(jnp.ones((c, c), bool), -1)
    decay = jnp.exp(jnp.where(incl, gc[..., :, None] - gc[..., None, :], -jnp.inf))
    kb = k * beta
    low = jnp.where(strict, jnp.einsum("bhnid,bhnjd->bhnij", kb, k) * decay, 0.0)
    rhs = jnp.concatenate([v * beta, kb * jnp.exp(gc)[..., None]], axis=-1)
    sol = lax.linalg.triangular_solve(low + jnp.eye(c, dtype=jnp.float32), rhs,
                                      left_side=True, lower=True, unit_diagonal=True)
    u, w = sol[..., :dv], sol[..., dv:]
    attn = jnp.einsum("bhnid,bhnjd->bhnij", q, k) * decay
    q_dec = q * jnp.exp(gc)[..., None]
    g_last = gc[..., -1]
    k_dec = k * jnp.exp(g_last[..., None] - gc)[..., None]

    def step(state, xs):
        q_i, k_i, u_i, w_i, a_i, gl_i = xs
        v_new = u_i - jnp.einsum("bhcd,bhde->bhce", w_i, state)
        o_i = jnp.einsum("bhcd,bhde->bhce", q_i, state) + jnp.einsum("bhij,bhje->bhie", a_i, v_new)
        state = state * jnp.exp(gl_i)[..., None, None] + jnp.einsum("bhcd,bhce->bhde", k_i, v_new)
        return state, o_i

    mv = lambda t: jnp.moveaxis(t, 2, 0)
    s0 = jnp.zeros(q.shape[:2] + (dk, dv), jnp.float32)
    _, o = lax.scan(step, s0, (mv(q_dec), mv(k_dec), mv(u), mv(w), mv(attn), mv(g_last)))
    return _from_chunks(jnp.moveaxis(o, 0, 2)).astype(dtype)


def hgrn2_recurrence(q, k, v, logf):
    dtype = v.dtype
    c = HGRN_CHUNK
    q, k, v, logf = (jnp.moveaxis(_to_chunks(t.astype(jnp.float32), c), 2, 0) for t in (q, k, v, logf))
    b = jnp.cumsum(logf, axis=-2)
    incl = jnp.tril(jnp.ones((c, c), bool))[:, :, None]

    def step(state, xs):
        q_i, k_i, v_i, b_i = xs
        b_last = b_i[..., -1:, :]
        decay = jnp.exp(jnp.where(incl, b_i[..., :, None, :] - b_i[..., None, :, :], -jnp.inf))
        attn = jnp.einsum("bhtd,bhsd,bhtsd->bhts", q_i, k_i, decay)
        o_i = (jnp.einsum("bhtd,bhde->bhte", q_i * jnp.exp(b_i), state)
               + jnp.einsum("bhts,bhse->bhte", attn, v_i))
        state = (state * jnp.exp(b_last)[..., 0, :, None]
                 + jnp.einsum("bhsd,bhse->bhde", k_i * jnp.exp(b_last - b_i), v_i))
        return state, o_i

    s0 = jnp.zeros(q.shape[1:3] + (q.shape[-1], v.shape[-1]), jnp.float32)
    _, o = lax.scan(step, s0, (q, k, v, b))
    return _from_chunks(jnp.moveaxis(o, 0, 2)).astype(dtype)


def conv_gdn_mixer(x, w_in, conv_a_w, conv_b_w, a_log, dt_bias, gdn_norm_g, w_out):
    bsz, s, _ = x.shape
    wa, wg, h = CONV_A_WIDTH, GDN_WIDTH, GDN_HEADS
    proj = x @ w_in
    cuts = [wa, 2 * wa, 3 * wa, 4 * wa, 4 * wa + 3 * wg, 4 * wa + 4 * wg, 4 * wa + 4 * wg + h]
    h_a, c_a, b_a, z_a, qkv, z_b, beta_raw, a_raw = jnp.split(proj, cuts, axis=-1)
    y_a = b_a * causal_depthwise_conv(c_a * h_a, conv_a_w) * jax.nn.silu(z_a)
    qkv = jax.nn.silu(causal_depthwise_conv(qkv, conv_b_w))
    q, k, v = (t.reshape(bsz, s, h, GDN_HEAD_DIM) for t in jnp.split(qkv, 3, axis=-1))
    q, k = l2_normalize(q), l2_normalize(k)
    beta = jax.nn.sigmoid(beta_raw.astype(jnp.float32))
    g = -jnp.exp(a_log.astype(jnp.float32)) * jax.nn.softplus(a_raw.astype(jnp.float32) + dt_bias.astype(jnp.float32))
    o = gated_delta_rule(q, k, v, g, beta)
    o = rms_norm(o, gdn_norm_g) * jax.nn.silu(z_b.reshape(bsz, s, h, GDN_HEAD_DIM))
    y = jnp.concatenate([y_a, o.reshape(bsz, s, wg)], axis=-1)
    return y @ w_out


def hgrn2_mixer(x, w_in, lower_bound, hgrn_norm_g, w_out):
    bsz, s, _ = x.shape
    q_raw, f_raw, i_in, z = jnp.split(x @ w_in, 4, axis=-1)
    f = lower_bound + (1.0 - lower_bound) * jax.nn.sigmoid(f_raw.astype(jnp.float32))
    q = jax.nn.silu(q_raw)
    k = (1.0 - f).astype(x.dtype)
    logf = jnp.log(f)
    heads = lambda t: t.reshape(bsz, s, HGRN_HEADS, HGRN_HEAD_DIM)
    o = hgrn2_recurrence(heads(q), heads(k), heads(i_in), heads(logf))
    o = rms_norm(o, hgrn_norm_g) * jax.nn.silu(heads(z))
    return o.reshape(bsz, s, HGRN_WIDTH) @ w_out


def setup_inputs(seed: int = 0) -> dict:
    key = jax.random.key(seed)
    ks = jax.random.split(key, 20)
    nrm = lambda k, shape, scale: jax.random.normal(k, shape, jnp.float32) * scale
    dt = jnp.exp(jax.random.uniform(ks[6], (N_EVEN, GDN_HEADS), jnp.float32)
                 * (math.log(0.1) - math.log(0.001)) + math.log(0.001))
    return {
        "x": nrm(ks[0], (BATCH, SEQ, D_MODEL), 1.0),
        "p": nrm(ks[1], (DEPTH, BATCH, SEQ, PL_DIM), 1.0),
        "w_in_even": nrm(ks[2], (N_EVEN, D_MODEL, EVEN_IN), D_MODEL ** -0.5),
        "conv_a_w": nrm(ks[3], (N_EVEN, CONV_A_KERNEL, CONV_A_WIDTH), CONV_A_KERNEL ** -0.5),
        "conv_b_w": nrm(ks[4], (N_EVEN, GDN_CONV_KERNEL, 3 * GDN_WIDTH), GDN_CONV_KERNEL ** -0.5),
        "a_log": jnp.log(jax.random.uniform(ks[5], (N_EVEN, GDN_HEADS), jnp.float32, 1.0, 16.0)),
        "dt_bias": dt + jnp.log(-jnp.expm1(-dt)),
        "gdn_norm_g": 1.0 + nrm(ks[7], (N_EVEN, GDN_HEAD_DIM), 0.02),
        "w_out_even": nrm(ks[8], (N_EVEN, EVEN_MIX, D_MODEL), EVEN_MIX ** -0.5 * DEEPNORM_BETA),
        "w_in_odd": nrm(ks[9], (N_ODD, D_MODEL, ODD_IN), D_MODEL ** -0.5),
        "lower_bounds": nrm(ks[10], (DEPTH, HGRN_WIDTH), 0.1),
        "hgrn_norm_g": 1.0 + nrm(ks[11], (N_ODD, HGRN_HEAD_DIM), 0.02),
        "w_out_odd": nrm(ks[12], (N_ODD, ODD_MIX, D_MODEL), ODD_MIX ** -0.5 * DEEPNORM_BETA),
        "ln_g": 1.0 + nrm(ks[13], (DEPTH, D_MODEL), 0.02),
        "ln_b": nrm(ks[14], (DEPTH, D_MODEL), 0.02),
        "w_pl": nrm(ks[15], (DEPTH, PL_DIM, D_MODEL), PL_DIM ** -0.5),
        "w_pl_gate": nrm(ks[16], (DEPTH, D_MODEL, D_MODEL), D_MODEL ** -0.5),
    }


def reference(x, p, w_in_even, conv_a_w, conv_b_w, a_log, dt_bias, gdn_norm_g, w_out_even,
              w_in_odd, lower_bounds, hgrn_norm_g, w_out_odd, ln_g, ln_b, w_pl, w_pl_gate):
    lbs = jnp.cumsum(jax.nn.softmax(lower_bounds.astype(jnp.float32), axis=0), axis=0)
    lbs = lbs - lbs[0]
    for i in range(DEPTH):
        j = i // 2
        if i % 2 == 0:
            s = conv_gdn_mixer(x, w_in_even[j], conv_a_w[j], conv_b_w[j], a_log[j], dt_bias[j],
                               gdn_norm_g[j], w_out_even[j])
        else:
            s = hgrn2_mixer(x, w_in_odd[j], lbs[i], hgrn_norm_g[j], w_out_odd[j])
        x = layer_norm(DEEPNORM_ALPHA * x + s, ln_g[i], ln_b[i])
        gate = jax.nn.sigmoid((x @ w_pl_gate[i]).astype(jnp.float32))
        x = x + ((p[i] @ w_pl[i]).astype(jnp.float32) * gate).astype(x.dtype)
    return x
```

```python
import functools
import math

import numpy as np
import jax
import jax.numpy as jnp
from jax import lax
from jax.experimental import pallas as pl
from jax.experimental.pallas import tpu as pltpu

F32 = jnp.float32
BF16 = jnp.bfloat16

D_MODEL = 1024
DEPTH = 2
PL_DIM = 256
CONV_A_WIDTH = 1024
CONV_A_KERNEL = 3
GDN_HEADS = 8
HEAD_DIM = 128
GDN_WIDTH = GDN_HEADS * HEAD_DIM
GDN_CONV_KERNEL = 4
HGRN_WIDTH = 2 * D_MODEL
HGRN_HEADS = HGRN_WIDTH // HEAD_DIM
DEEPNORM_ALPHA = (2.0 * DEPTH) ** 0.25
NORM_EPS = 1e-5
L2_EPS = 1e-6

LANES = 128
SUBLANES = 8
CHUNK = 64
HALO = SUBLANES
NEG_BIG = -1e30

VMEM_LIMIT = 56 * 1024 * 1024


def _sigmoid(x):
    return 1.0 / (1.0 + jnp.exp(-x))


def _silu(x):
    return x * _sigmoid(x)


def _softplus(x):
    return jnp.maximum(x, 0.0) + jnp.log1p(jnp.exp(-jnp.abs(x)))


def _mm(a, b):
    return jnp.dot(a.astype(BF16), b.astype(BF16), preferred_element_type=F32)


def _mm_nt(a, b):
    return lax.dot_general(a.astype(BF16), b.astype(BF16), (((1,), (1,)), ((), ())),
                           preferred_element_type=F32)


def _mm_tn(a, b):
    return lax.dot_general(a.astype(BF16), b.astype(BF16), (((0,), (0,)), ((), ())),
                           preferred_element_type=F32)


def _mm_f32(a, b):
    return jnp.dot(a, b, preferred_element_type=F32, precision=lax.Precision.HIGHEST)


def _const_spec(shape):
    nd = len(shape)
    return pl.BlockSpec(shape, lambda *_: (0,) * nd, pipeline_mode=pl.Buffered(1))


def _pair_mask(row, col, log2_b):
    return (jnp.right_shift(jnp.bitwise_xor(row, col), log2_b) == 1) & (row > col)


def _even_in_body(x_ref, wa_ref, wqkv_ref, wzb_ref, wba_ref, ca_ref, cb_ref,
                  ya_ref, q_ref, k_ref, v_ref, zg_ref, ba_ref, ua_scr, qkv_scr, *, tm):
    @pl.when(pl.program_id(1) == 0)
    def _():
        ua_scr[0:HALO, :] = jnp.zeros((HALO, CONV_A_WIDTH), F32)
        qkv_scr[0:HALO, :] = jnp.zeros((HALO, 3 * GDN_WIDTH), F32)

    xt = x_ref[0].astype(BF16)

    pa = jnp.dot(xt, wa_ref[...], preferred_element_type=F32)
    for j in range(CONV_A_WIDTH // LANES):
        sl = slice(LANES * j, LANES * (j + 1))
        h = pa[:, LANES * j:LANES * (j + 1)]
        c = pa[:, CONV_A_WIDTH + LANES * j:CONV_A_WIDTH + LANES * (j + 1)]
        b = pa[:, 2 * CONV_A_WIDTH + LANES * j:2 * CONV_A_WIDTH + LANES * (j + 1)]
        z = pa[:, 3 * CONV_A_WIDTH + LANES * j:3 * CONV_A_WIDTH + LANES * (j + 1)]
        u = c * h
        ua_scr[HALO:HALO + tm, sl] = u
        conv = (ca_ref[0:1, sl] * ua_scr[HALO - 2:HALO - 2 + tm, sl]
                + ca_ref[1:2, sl] * ua_scr[HALO - 1:HALO - 1 + tm, sl]
                + ca_ref[2:3, sl] * u)
        ya_ref[0, :, sl] = (b * conv * _silu(z)).astype(BF16)
        ua_scr[0:HALO, sl] = ua_scr[tm:tm + HALO, sl]

    pq = jnp.dot(xt, wqkv_ref[...], preferred_element_type=F32)
    for j in range(3 * GDN_WIDTH // LANES):
        sl = slice(LANES * j, LANES * (j + 1))
        cur = pq[:, LANES * j:LANES * (j + 1)]
        qkv_scr[HALO:HALO + tm, sl] = cur
        conv = (cb_ref[0:1, sl] * qkv_scr[HALO - 3:HALO - 3 + tm, sl]
                + cb_ref[1:2, sl] * qkv_scr[HALO - 2:HALO - 2 + tm, sl]
                + cb_ref[2:3, sl] * qkv_scr[HALO - 1:HALO - 1 + tm, sl]
                + cb_ref[3:4, sl] * cur)
        a = _silu(conv)
        qkv_scr[0:HALO, sl] = qkv_scr[tm:tm + HALO, sl]
        part, hj = divmod(j, GDN_HEADS)
        hs = slice(LANES * hj, LANES * (hj + 1))
        if part < 2:
            a = a * lax.rsqrt(jnp.sum(a * a, axis=-1, keepdims=True) + L2_EPS)
            (q_ref if part == 0 else k_ref)[0, :, hs] = a.astype(BF16)
        else:
            v_ref[0, :, hs] = a.astype(BF16)

    zg_ref[0] = _silu(jnp.dot(xt, wzb_ref[...], preferred_element_type=F32)).astype(BF16)
    ba_ref[0] = jnp.dot(xt, wba_ref[...], preferred_element_type=F32)


def _even_in_call(x, wa, wqkv, wzb, wba, ca, cb, *, tm):
    bsz, s, _ = x.shape
    row = lambda w: pl.BlockSpec((1, tm, w), lambda b, t: (b, t, 0))
    return pl.pallas_call(
        functools.partial(_even_in_body, tm=tm),
        grid=(bsz, s // tm),
        in_specs=[row(D_MODEL), _const_spec(wa.shape), _const_spec(wqkv.shape), _const_spec(wzb.shape),
                  _const_spec(wba.shape), _const_spec(ca.shape), _const_spec(cb.shape)],
        out_specs=[row(CONV_A_WIDTH), row(GDN_WIDTH), row(GDN_WIDTH), row(GDN_WIDTH), row(GDN_WIDTH),
                   row(LANES)],
        out_shape=[jax.ShapeDtypeStruct((bsz, s, CONV_A_WIDTH), BF16),
                   jax.ShapeDtypeStruct((bsz, s, GDN_WIDTH), BF16),
                   jax.ShapeDtypeStruct((bsz, s, GDN_WIDTH), BF16),
                   jax.ShapeDtypeStruct((bsz, s, GDN_WIDTH), BF16),
                   jax.ShapeDtypeStruct((bsz, s, GDN_WIDTH), BF16),
                   jax.ShapeDtypeStruct((bsz, s, LANES), F32)],
        scratch_shapes=[pltpu.VMEM((tm + HALO, CONV_A_WIDTH), F32),
                        pltpu.VMEM((tm + HALO, 3 * GDN_WIDTH), F32)],
        compiler_params=pltpu.CompilerParams(dimension_semantics=("arbitrary", "arbitrary"),
                                             vmem_limit_bytes=VMEM_LIMIT),
        name="even_in_proj",
    )(x, wa, wqkv, wzb, wba, ca, cb)


def _gdn_body(q_ref, k_ref, v_ref, zg_ref, ba_ref, bat_ref, arow_ref, acol_ref, gn_ref,
              o_ref, s_scr, *, tb):
    nchunk = tb // CHUNK

    @pl.when(pl.program_id(1) == 0)
    def _():
        s_scr[...] = jnp.zeros(s_scr.shape, F32)

    ba = ba_ref[0]
    beta_cols = _sigmoid(ba)
    g_cols = -jnp.exp(arow_ref[0:1, :]) * _softplus(ba + arow_ref[1:2, :])
    g_rows = -jnp.exp(acol_ref[:, 0:1]) * _softplus(bat_ref[0] + acol_ref[:, 1:2])
    ti = lax.broadcasted_iota(jnp.int32, (tb, tb), 0)
    tj = lax.broadcasted_iota(jnp.int32, (tb, tb), 1)
    same_chunk = (ti // CHUNK) == (tj // CHUNK)
    lower_blk = jnp.where(same_chunk & (tj <= ti), 1.0, 0.0).astype(F32)
    upper_blk = jnp.where(same_chunk & (ti <= tj), 1.0, 0.0).astype(F32)
    gc_cols = _mm_f32(lower_blk, g_cols)
    gc_rows = _mm_f32(g_rows, upper_blk)

    row = lax.broadcasted_iota(jnp.int32, (CHUNK, CHUNK), 0)
    col = lax.broadcasted_iota(jnp.int32, (CHUNK, CHUNK), 1)
    incl = row >= col
    eye = jnp.where(row == col, 1.0, 0.0).astype(F32)
    level_masks = [_pair_mask(row, col, lb) for lb in range(int(math.log2(CHUNK)))]
    scale = HEAD_DIM ** -0.5
    gn = gn_ref[...]

    for h in range(GDN_HEADS):
        hs = slice(LANES * h, LANES * (h + 1))
        state = s_scr[h]
        for ci in range(nchunk):
            rs = slice(CHUNK * ci, CHUNK * (ci + 1))
            q = q_ref[0, rs, hs].astype(F32) * scale
            k_b = k_ref[0, rs, hs]
            k = k_b.astype(F32)
            v = v_ref[0, rs, hs].astype(F32)
            beta = beta_cols[rs, h:h + 1]
            gcc = gc_cols[rs, GDN_HEADS + h:GDN_HEADS + h + 1]
            gcr = gc_rows[GDN_HEADS + h:GDN_HEADS + h + 1, rs]
            g_last = gcc[CHUNK - 1:CHUNK, :]
            decay = jnp.exp(jnp.where(incl, gcc - gcr, NEG_BIG))
            e_gc = jnp.exp(gcc)

            kb = k * beta
            low = jnp.where(row > col, _mm_nt(kb, k_b) * decay, 0.0)
            attn = _mm_nt(q, k_b) * decay

            tinv = eye - jnp.where(level_masks[0], low, 0.0)
            for lb in range(1, len(level_masks)):
                p = _mm(tinv, jnp.where(level_masks[lb], low, 0.0))
                tinv = tinv - _mm(p, tinv)

            rhs = jnp.concatenate([v * beta, kb * e_gc], axis=-1)
            uw = _mm(tinv, rhs)
            u, w = uw[:, :HEAD_DIM], uw[:, HEAD_DIM:]

            v_new = u - _mm(w, state)
            o = _mm(q * e_gc, state) + _mm(attn, v_new)
            state = state * jnp.exp(g_last) + _mm_tn(k * jnp.exp(g_last - gcc), v_new)

            o = o * lax.rsqrt(jnp.mean(o * o, axis=-1, keepdims=True) + NORM_EPS) * gn
            o_ref[0, rs, hs] = (o * zg_ref[0, rs, hs].astype(F32)).astype(BF16)
        s_scr[h] = state


def _gdn_call(q, k, v, zg, ba, bat, arow, acol, gn, *, tb):
    bsz, s, _ = q.shape
    row = lambda w: pl.BlockSpec((1, tb, w), lambda b, t: (b, t, 0))
    return pl.pallas_call(
        functools.partial(_gdn_body, tb=tb),
        grid=(bsz, s // tb),
        in_specs=[row(GDN_WIDTH), row(GDN_WIDTH), row(GDN_WIDTH), row(GDN_WIDTH), row(LANES),
                  pl.BlockSpec((1, 2 * GDN_HEADS, tb), lambda b, t: (b, 0, t)),
                  _const_spec(arow.shape), _const_spec(acol.shape), _const_spec(gn.shape)],
        out_specs=row(GDN_WIDTH),
        out_shape=jax.ShapeDtypeStruct((bsz, s, GDN_WIDTH), BF16),
        scratch_shapes=[pltpu.VMEM((GDN_HEADS, HEAD_DIM, HEAD_DIM), F32)],
        compiler_params=pltpu.CompilerParams(dimension_semantics=("arbitrary", "arbitrary"),
                                             vmem_limit_bytes=VMEM_LIMIT),
        name="gated_delta_rule",
    )(q, k, v, zg, ba, bat, arow, acol, gn)


def _out_body(*refs, n_y):
    y_refs, (x_ref, p_ref) = refs[:n_y], refs[n_y:n_y + 2]
    wo_refs = refs[n_y + 2:2 * n_y + 2]
    wg_ref, wp_ref, lg_ref, lb_ref, o_ref = refs[2 * n_y + 2:]
    s = jnp.dot(y_refs[0][...], wo_refs[0][...], preferred_element_type=F32)
    for y_ref, wo_ref in zip(y_refs[1:], wo_refs[1:]):
        s = s + jnp.dot(y_ref[...], wo_ref[...], preferred_element_type=F32)
    t = DEEPNORM_ALPHA * x_ref[...] + s
    mu = jnp.mean(t, axis=-1, keepdims=True)
    tc = t - mu
    var = jnp.mean(tc * tc, axis=-1, keepdims=True)
    xn = tc * lax.rsqrt(var + NORM_EPS) * lg_ref[...] + lb_ref[...]
    gate = _sigmoid(jnp.dot(xn.astype(BF16), wg_ref[...], preferred_element_type=F32))
    emb = jnp.dot(p_ref[...].astype(BF16), wp_ref[...], preferred_element_type=F32)
    o_ref[...] = xn + emb * gate


def _out_call(ys, x, p, wos, wg, wp, lg, lb, *, tm):
    n, d = x.shape
    row = lambda w: pl.BlockSpec((tm, w), lambda i: (i, 0))
    return pl.pallas_call(
        functools.partial(_out_body, n_y=len(ys)),
        grid=(n // tm,),
        in_specs=([row(y.shape[1]) for y in ys] + [row(d), row(p.shape[1])]
                  + [_const_spec(w.shape) for w in wos]
                  + [_const_spec(wg.shape), _const_spec(wp.shape), _const_spec(lg.shape), _const_spec(lb.shape)]),
        out_specs=row(d),
        out_shape=jax.ShapeDtypeStruct((n, d), F32),
        compiler_params=pltpu.CompilerParams(dimension_semantics=("arbitrary",),
                                             vmem_limit_bytes=VMEM_LIMIT),
        name="out_proj_norm_gate",
    )(*ys, x, p, *wos, wg, wp, lg, lb)


def _odd_in_body(x_ref, w_ref, lbr_ref, q_ref, f_ref, v_ref, zg_ref):
    xt = x_ref[...].astype(BF16)
    lb_raw = lbr_ref[...]
    e = jnp.exp(lb_raw - jnp.max(lb_raw, axis=0, keepdims=True))
    sm = e / jnp.sum(e, axis=0, keepdims=True)
    lower = (sm[0:1, :] + sm[1:2, :]) - sm[0:1, :]
    w = HGRN_WIDTH
    q_ref[...] = _silu(jnp.dot(xt, w_ref[:, 0:w], preferred_element_type=F32)).astype(BF16)
    f_raw = jnp.dot(xt, w_ref[:, w:2 * w], preferred_element_type=F32)
    f_ref[...] = lower + (1.0 - lower) * _sigmoid(f_raw)
    v_ref[...] = jnp.dot(xt, w_ref[:, 2 * w:3 * w], preferred_element_type=F32).astype(BF16)
    zg_ref[...] = _silu(jnp.dot(xt, w_ref[:, 3 * w:4 * w], preferred_element_type=F32)).astype(BF16)


def _odd_in_call(x, w, lbr, *, tm):
    n, d = x.shape
    row = lambda wd: pl.BlockSpec((tm, wd), lambda i: (i, 0))
    return pl.pallas_call(
        _odd_in_body,
        grid=(n // tm,),
        in_specs=[row(d), _const_spec(w.shape), _const_spec(lbr.shape)],
        out_specs=[row(HGRN_WIDTH)] * 4,
        out_shape=[jax.ShapeDtypeStruct((n, HGRN_WIDTH), BF16),
                   jax.ShapeDtypeStruct((n, HGRN_WIDTH), F32),
                   jax.ShapeDtypeStruct((n, HGRN_WIDTH), BF16),
                   jax.ShapeDtypeStruct((n, HGRN_WIDTH), BF16)],
        compiler_params=pltpu.CompilerParams(dimension_semantics=("arbitrary",),
                                             vmem_limit_bytes=VMEM_LIMIT),
        name="odd_in_proj",
    )(x, w, lbr)


def _segment_matrix():
    t = np.arange(CHUNK)[:, None]
    j = np.arange(CHUNK)[None, :]
    blocks = []
    for lb in range(int(math.log2(CHUNK))):
        b = 1 << lb
        start, end = (t // b) * b, (t // b) * b + b - 1
        upper = ((t // b) % 2) == 1
        blocks.append(np.where(upper, (j >= start) & (j <= t), (j > t) & (j <= end)))
    blocks.append(j <= t)
    blocks.append(j > t)
    return np.concatenate(blocks, axis=0).astype(np.float32)


def _hgrn_body(q_ref, f_ref, v_ref, zg_ref, seg_ref, gn_ref, o_ref, s_scr, *, tb, hb):
    nchunk = tb // CHUNK
    nlev = int(math.log2(CHUNK))

    @pl.when(pl.program_id(2) == 0)
    def _():
        s_scr[...] = jnp.zeros(s_scr.shape, F32)

    row = lax.broadcasted_iota(jnp.int32, (CHUNK, CHUNK), 0)
    col = lax.broadcasted_iota(jnp.int32, (CHUNK, CHUNK), 1)
    level_masks = [_pair_mask(row, col, lb) for lb in range(nlev)]
    trow = lax.broadcasted_iota(jnp.int32, (CHUNK, LANES), 0)
    upper_rows = [(jnp.right_shift(trow, lb) & 1) == 1 for lb in range(nlev)]
    seg = seg_ref[...]
    gn = gn_ref[...]

    for h in range(hb):
        hs = slice(LANES * h, LANES * (h + 1))
        state_t = s_scr[h]
        for ci in range(nchunk):
            rs = slice(CHUNK * ci, CHUNK * (ci + 1))
            f = f_ref[0, rs, hs]
            q = q_ref[0, rs, hs].astype(F32)
            k = 1.0 - f
            v_b = v_ref[0, rs, hs]
            l2f = jnp.log2(f)
            l2f_hi = l2f.astype(BF16)
            l2f_lo = (l2f - l2f_hi.astype(F32)).astype(BF16)
            sums = (jnp.dot(seg, l2f_hi, preferred_element_type=F32)
                    + jnp.dot(seg, l2f_lo, preferred_element_type=F32))
            fac = jnp.exp2(sums)

            attn = jnp.where(row == col, _mm_nt(q, k), 0.0)
            for lb in range(nlev):
                p = jnp.where(upper_rows[lb], q, k) * fac[CHUNK * lb:CHUNK * (lb + 1)]
                attn = jnp.where(level_masks[lb], _mm_nt(p, p), attn)
            inc = fac[CHUNK * nlev:CHUNK * (nlev + 1)]
            exc = fac[CHUNK * (nlev + 1):CHUNK * (nlev + 2)]

            o = _mm(attn, v_b) + _mm_nt(q * inc, state_t)
            state_t = state_t * inc[CHUNK - 1:CHUNK, :] + _mm_tn(v_b, k * exc)

            o = o * lax.rsqrt(jnp.mean(o * o, axis=-1, keepdims=True) + NORM_EPS) * gn
            o_ref[0, rs, hs] = (o * zg_ref[0, rs, hs].astype(F32)).astype(BF16)
        s_scr[h] = state_t


def _hgrn_call(q, f, v, zg, seg, gn, *, tb, hb):
    bsz, s, w = q.shape
    blk = lambda: pl.BlockSpec((1, tb, hb * LANES), lambda b, g, t: (b, t, g))
    return pl.pallas_call(
        functools.partial(_hgrn_body, tb=tb, hb=hb),
        grid=(bsz, w // (hb * LANES), s // tb),
        in_specs=[blk(), blk(), blk(), blk(), _const_spec(seg.shape), _const_spec(gn.shape)],
        out_specs=blk(),
        out_shape=jax.ShapeDtypeStruct((bsz, s, w), BF16),
        scratch_shapes=[pltpu.VMEM((hb, HEAD_DIM, HEAD_DIM), F32)],
        compiler_params=pltpu.CompilerParams(dimension_semantics=("arbitrary", "arbitrary", "arbitrary"),
                                             vmem_limit_bytes=VMEM_LIMIT),
        name="hgrn2_recurrence",
    )(q, f, v, zg, seg, gn)


def kernel(x, p, w_in_even, conv_a_w, conv_b_w, a_log, dt_bias, gdn_norm_g, w_out_even, w_in_odd,
           lower_bounds, hgrn_norm_g, w_out_odd, ln_g, ln_b, w_pl, w_pl_gate):
    bsz, s, d = x.shape
    n = bsz * s
    wa_end = 4 * CONV_A_WIDTH
    qkv_end = wa_end + 3 * GDN_WIDTH
    zb_end = qkv_end + GDN_WIDTH

    w_in = w_in_even[0].astype(BF16)
    wba = jnp.pad(w_in[:, zb_end:], ((0, 0), (0, LANES - 2 * GDN_HEADS)))
    ya, q, k, v, zg, ba = _even_in_call(
        x, w_in[:, :wa_end], w_in[:, wa_end:qkv_end], w_in[:, qkv_end:zb_end], wba,
        conv_a_w[0], conv_b_w[0], tm=256)

    pad_lane = lambda a: jnp.pad(a, (GDN_HEADS, LANES - 2 * GDN_HEADS))
    arow = jnp.stack([pad_lane(a_log[0]), pad_lane(dt_bias[0])], axis=0)
    acol = jnp.stack([jnp.pad(a_log[0], (GDN_HEADS, 0)), jnp.pad(dt_bias[0], (GDN_HEADS, 0))], axis=1)
    bat = jnp.swapaxes(ba[:, :, :2 * GDN_HEADS], 1, 2)
    og = _gdn_call(q, k, v, zg, ba, bat, arow, acol, gdn_norm_g[0][None, :], tb=128)

    w_out = w_out_even[0].astype(BF16)
    x1 = _out_call([ya.reshape(n, -1), og.reshape(n, -1)], x.reshape(n, d), p[0].reshape(n, PL_DIM),
                   [w_out[:CONV_A_WIDTH], w_out[CONV_A_WIDTH:]], w_pl_gate[0].astype(BF16),
                   w_pl[0].astype(BF16), ln_g[0][None, :], ln_b[0][None, :], tm=256)

    hq, hf, hv, hzg = _odd_in_call(x1, w_in_odd[0].astype(BF16), lower_bounds, tm=256)
    shp = (bsz, s, HGRN_WIDTH)
    seg = jnp.asarray(_segment_matrix(), BF16)
    ho = _hgrn_call(hq.reshape(shp), hf.reshape(shp), hv.reshape(shp), hzg.reshape(shp), seg,
                    hgrn_norm_g[0][None, :], tb=256, hb=4)
    out = _out_call([ho.reshape(n, -1)], x1, p[1].reshape(n, PL_DIM), [w_out_odd[0].astype(BF16)],
                    w_pl_gate[1].astype(BF16), w_pl[1].astype(BF16), ln_g[1][None, :], ln_b[1][None, :],
                    tm=256)
    return out.reshape(bsz, s, d)
```

```python
import functools
import math

import numpy as np
import jax
import jax.numpy as jnp
from jax import lax
from jax.experimental import pallas as pl
from jax.experimental.pallas import tpu as pltpu

F32 = jnp.float32
BF16 = jnp.bfloat16

D_MODEL = 1024
DEPTH = 2
PL_DIM = 256
CONV_A_WIDTH = 1024
CONV_A_KERNEL = 3
GDN_HEADS = 8
HEAD_DIM = 128
GDN_WIDTH = GDN_HEADS * HEAD_DIM
GDN_CONV_KERNEL = 4
HGRN_WIDTH = 2 * D_MODEL
HGRN_HEADS = HGRN_WIDTH // HEAD_DIM
DEEPNORM_ALPHA = (2.0 * DEPTH) ** 0.25
NORM_EPS = 1e-5
L2_EPS = 1e-6

LANES = 128
SUBLANES = 8
CHUNK = 64
HALO = SUBLANES
NEG_BIG = -1e30

VMEM_LIMIT = 56 * 1024 * 1024


def _sigmoid(x):
    return 1.0 / (1.0 + jnp.exp(-x))


def _silu(x):
    return x * _sigmoid(x)


def _softplus(x):
    return jnp.maximum(x, 0.0) + jnp.log1p(jnp.exp(-jnp.abs(x)))


def _mm(a, b):
    return jnp.dot(a.astype(BF16), b.astype(BF16), preferred_element_type=F32)


def _mm_nt(a, b):
    return lax.dot_general(a.astype(BF16), b.astype(BF16), (((1,), (1,)), ((), ())),
                           preferred_element_type=F32)


def _mm_tn(a, b):
    return lax.dot_general(a.astype(BF16), b.astype(BF16), (((0,), (0,)), ((), ())),
                           preferred_element_type=F32)


def _mm_f32(a, b):
    return jnp.dot(a, b, preferred_element_type=F32, precision=lax.Precision.HIGHEST)


def _const_spec(shape):
    nd = len(shape)
    return pl.BlockSpec(shape, lambda *_: (0,) * nd, pipeline_mode=pl.Buffered(1))


def _pair_mask(row, col, log2_b):
    return (jnp.right_shift(jnp.bitwise_xor(row, col), log2_b) == 1) & (row > col)


def _even_in_body(x_ref, wa_ref, wqkv_ref, wzb_ref, wba_ref, ca_ref, cb_ref,
                  ya_ref, q_ref, k_ref, v_ref, zg_ref, ba_ref, ua_scr, qkv_scr, *, tm):
    @pl.when(pl.program_id(1) == 0)
    def _():
        ua_scr[0:HALO, :] = jnp.zeros((HALO, CONV_A_WIDTH), F32)
        qkv_scr[0:HALO, :] = jnp.zeros((HALO, 3 * GDN_WIDTH), F32)

    xt = x_ref[0].astype(BF16)

    pa = jnp.dot(xt, wa_ref[...], preferred_element_type=F32)
    for j in range(CONV_A_WIDTH // LANES):
        sl = slice(LANES * j, LANES * (j + 1))
        h = pa[:, LANES * j:LANES * (j + 1)]
        c = pa[:, CONV_A_WIDTH + LANES * j:CONV_A_WIDTH + LANES * (j + 1)]
        b = pa[:, 2 * CONV_A_WIDTH + LANES * j:2 * CONV_A_WIDTH + LANES * (j + 1)]
        z = pa[:, 3 * CONV_A_WIDTH + LANES * j:3 * CONV_A_WIDTH + LANES * (j + 1)]
        u = c * h
        ua_scr[HALO:HALO + tm, sl] = u
        conv = (ca_ref[0:1, sl] * ua_scr[HALO - 2:HALO - 2 + tm, sl]
                + ca_ref[1:2, sl] * ua_scr[HALO - 1:HALO - 1 + tm, sl]
                + ca_ref[2:3, sl] * u)
        ya_ref[0, :, sl] = (b * conv * _silu(z)).astype(BF16)
        ua_scr[0:HALO, sl] = ua_scr[tm:tm + HALO, sl]

    pq = jnp.dot(xt, wqkv_ref[...], preferred_element_type=F32)
    for j in range(3 * GDN_WIDTH // LANES):
        sl = slice(LANES * j, LANES * (j + 1))
        cur = pq[:, LANES * j:LANES * (j + 1)]
        qkv_scr[HALO:HALO + tm, sl] = cur
        conv = (cb_ref[0:1, sl] * qkv_scr[HALO - 3:HALO - 3 + tm, sl]
                + cb_ref[1:2, sl] * qkv_scr[HALO - 2:HALO - 2 + tm, sl]
                + cb_ref[2:3, sl] * qkv_scr[HALO - 1:HALO - 1 + tm, sl]
                + cb_ref[3:4, sl] * cur)
        a = _silu(conv)
        qkv_scr[0:HALO, sl] = qkv_scr[tm:tm + HALO, sl]
        part, hj = divmod(j, GDN_HEADS)
        hs = slice(LANES * hj, LANES * (hj + 1))
        if part < 2:
            a = a * lax.rsqrt(jnp.sum(a * a, axis=-1, keepdims=True) + L2_EPS)
            (q_ref if part == 0 else k_ref)[0, :, hs] = a.astype(BF16)
        else:
            v_ref[0, :, hs] = a.astype(BF16)

    zg_ref[0] = _silu(jnp.dot(xt, wzb_ref[...], preferred_element_type=F32)).astype(BF16)
    ba_ref[0] = jnp.dot(xt, wba_ref[...], preferred_element_type=F32)


def _even_in_call(x, wa, wqkv, wzb, wba, ca, cb, *, tm):
    bsz, s, _ = x.shape
    row = lambda w: pl.BlockSpec((1, tm, w), lambda b, t: (b, t, 0))
    return pl.pallas_call(
        functools.partial(_even_in_body, tm=tm),
        grid=(bsz, s // tm),
        in_specs=[row(D_MODEL), _const_spec(wa.shape), _const_spec(wqkv.shape), _const_spec(wzb.shape),
                  _const_spec(wba.shape), _const_spec(ca.shape), _const_spec(cb.shape)],
        out_specs=[row(CONV_A_WIDTH), row(GDN_WIDTH), row(GDN_WIDTH), row(GDN_WIDTH), row(GDN_WIDTH),
                   row(LANES)],
        out_shape=[jax.ShapeDtypeStruct((bsz, s, CONV_A_WIDTH), BF16),
                   jax.ShapeDtypeStruct((bsz, s, GDN_WIDTH), BF16),
                   jax.ShapeDtypeStruct((bsz, s, GDN_WIDTH), BF16),
                   jax.ShapeDtypeStruct((bsz, s, GDN_WIDTH), BF16),
                   jax.ShapeDtypeStruct((bsz, s, GDN_WIDTH), BF16),
                   jax.ShapeDtypeStruct((bsz, s, LANES), F32)],
        scratch_shapes=[pltpu.VMEM((tm + HALO, CONV_A_WIDTH), F32),
                        pltpu.VMEM((tm + HALO, 3 * GDN_WIDTH), F32)],
        compiler_params=pltpu.CompilerParams(dimension_semantics=("arbitrary", "arbitrary"),
                                             vmem_limit_bytes=VMEM_LIMIT),
        name="even_in_proj",
    )(x, wa, wqkv, wzb, wba, ca, cb)


def _gdn_body(q_ref, k_ref, v_ref, zg_ref, ba_ref, bat_ref, arow_ref, acol_ref, gn_ref,
              o_ref, s_scr, *, tb):
    nchunk = tb // CHUNK

    @pl.when(pl.program_id(1) == 0)
    def _():
        s_scr[...] = jnp.zeros(s_scr.shape, F32)

    ba = ba_ref[0]
    beta_cols = _sigmoid(ba)
    g_cols = -jnp.exp(arow_ref[0:1, :]) * _softplus(ba + arow_ref[1:2, :])
    g_rows = -jnp.exp(acol_ref[:, 0:1]) * _softplus(bat_ref[0] + acol_ref[:, 1:2])
    ti = lax.broadcasted_iota(jnp.int32, (tb, tb), 0)
    tj = lax.broadcasted_iota(jnp.int32, (tb, tb), 1)
    same_chunk = (ti // CHUNK) == (tj // CHUNK)
    lower_blk = jnp.where(same_chunk & (tj <= ti), 1.0, 0.0).astype(F32)
    upper_blk = jnp.where(same_chunk & (ti <= tj), 1.0, 0.0).astype(F32)
    gc_cols = _mm_f32(lower_blk, g_cols)
    gc_rows = _mm_f32(g_rows, upper_blk)

    row = lax.broadcasted_iota(jnp.int32, (CHUNK, CHUNK), 0)
    col = lax.broadcasted_iota(jnp.int32, (CHUNK, CHUNK), 1)
    incl = row >= col
    eye = jnp.where(row == col, 1.0, 0.0).astype(F32)
    level_masks = [_pair_mask(row, col, lb) for lb in range(int(math.log2(CHUNK)))]
    scale = HEAD_DIM ** -0.5
    gn = gn_ref[...]

    probs = [(h, ci) for h in range(GDN_HEADS) for ci in range(nchunk)]
    pr = []
    for h, ci in probs:
        hs = slice(LANES * h, LANES * (h + 1))
        rs = slice(CHUNK * ci, CHUNK * (ci + 1))
        d = dict(hs=hs, rs=rs)
        q = q_ref[0, rs, hs].astype(F32) * scale
        d["k_b"] = k_ref[0, rs, hs]
        k = d["k_b"].astype(F32)
        v = v_ref[0, rs, hs].astype(F32)
        beta = beta_cols[rs, h:h + 1]
        gcc = gc_cols[rs, GDN_HEADS + h:GDN_HEADS + h + 1]
        gcr = gc_rows[GDN_HEADS + h:GDN_HEADS + h + 1, rs]
        g_last = gcc[CHUNK - 1:CHUNK, :]
        d["decay"] = jnp.exp(jnp.where(incl, gcc - gcr, NEG_BIG))
        e_gc = jnp.exp(gcc)
        kb = k * beta
        d["kbq"] = jnp.concatenate([kb, q], axis=0).astype(BF16)
        d["rhs"] = jnp.concatenate([v * beta, kb * e_gc], axis=-1).astype(BF16)
        d["q_dec"] = (q * e_gc).astype(BF16)
        d["k_dec"] = (k * jnp.exp(g_last - gcc)).astype(BF16)
        d["s_decay"] = jnp.exp(g_last)
        pr.append(d)

    for d in pr:
        both = _mm_nt(d["kbq"], d["k_b"])
        d["low"] = jnp.where(row > col, both[:CHUNK] * d["decay"], 0.0)
        d["attn"] = (both[CHUNK:] * d["decay"]).astype(BF16)
        d["tinv"] = eye - jnp.where(level_masks[0], d["low"], 0.0)
    for lb in range(1, len(level_masks)):
        for d in pr:
            d["p"] = _mm(d["tinv"], jnp.where(level_masks[lb], d["low"], 0.0))
        for d in pr:
            d["tinv"] = d["tinv"] - _mm(d["p"], d["tinv"])
    for d in pr:
        uw = _mm(d["tinv"], d["rhs"])
        d["u"], d["w"] = uw[:, :HEAD_DIM], uw[:, HEAD_DIM:].astype(BF16)

    states = [s_scr[h] for h in range(GDN_HEADS)]
    for ci in range(nchunk):
        cur = [d for d, (h, c) in zip(pr, probs) if c == ci]
        for h, d in enumerate(cur):
            ws = _mm(jnp.concatenate([d["w"], d["q_dec"]], axis=0), states[h])
            d["v_new"] = (d["u"] - ws[:CHUNK]).astype(BF16)
            d["o_inter"] = ws[CHUNK:]
        for h, d in enumerate(cur):
            o = d["o_inter"] + _mm(d["attn"], d["v_new"])
            states[h] = states[h] * d["s_decay"] + _mm_tn(d["k_dec"], d["v_new"])
            o = o * lax.rsqrt(jnp.mean(o * o, axis=-1, keepdims=True) + NORM_EPS) * gn
            o_ref[0, d["rs"], d["hs"]] = (o * zg_ref[0, d["rs"], d["hs"]].astype(F32)).astype(BF16)
    for h in range(GDN_HEADS):
        s_scr[h] = states[h]


def _gdn_call(q, k, v, zg, ba, bat, arow, acol, gn, *, tb):
    bsz, s, _ = q.shape
    row = lambda w: pl.BlockSpec((1, tb, w), lambda b, t: (b, t, 0))
    return pl.pallas_call(
        functools.partial(_gdn_body, tb=tb),
        grid=(bsz, s // tb),
        in_specs=[row(GDN_WIDTH), row(GDN_WIDTH), row(GDN_WIDTH), row(GDN_WIDTH), row(LANES),
                  pl.BlockSpec((1, 2 * GDN_HEADS, tb), lambda b, t: (b, 0, t)),
                  _const_spec(arow.shape), _const_spec(acol.shape), _const_spec(gn.shape)],
        out_specs=row(GDN_WIDTH),
        out_shape=jax.ShapeDtypeStruct((bsz, s, GDN_WIDTH), BF16),
        scratch_shapes=[pltpu.VMEM((GDN_HEADS, HEAD_DIM, HEAD_DIM), F32)],
        compiler_params=pltpu.CompilerParams(dimension_semantics=("arbitrary", "arbitrary"),
                                             vmem_limit_bytes=VMEM_LIMIT),
        name="gated_delta_rule",
    )(q, k, v, zg, ba, bat, arow, acol, gn)


def _out_body(*refs, n_y):
    y_refs, (x_ref, p_ref) = refs[:n_y], refs[n_y:n_y + 2]
    wo_refs = refs[n_y + 2:2 * n_y + 2]
    wg_ref, wp_ref, lg_ref, lb_ref, o_ref = refs[2 * n_y + 2:]
    s = jnp.dot(y_refs[0][...], wo_refs[0][...], preferred_element_type=F32)
    for y_ref, wo_ref in zip(y_refs[1:], wo_refs[1:]):
        s = s + jnp.dot(y_ref[...], wo_ref[...], preferred_element_type=F32)
    t = DEEPNORM_ALPHA * x_ref[...] + s
    mu = jnp.mean(t, axis=-1, keepdims=True)
    tc = t - mu
    var = jnp.mean(tc * tc, axis=-1, keepdims=True)
    xn = tc * lax.rsqrt(var + NORM_EPS) * lg_ref[...] + lb_ref[...]
    gate = _sigmoid(jnp.dot(xn.astype(BF16), wg_ref[...], preferred_element_type=F32))
    emb = jnp.dot(p_ref[...].astype(BF16), wp_ref[...], preferred_element_type=F32)
    o_ref[...] = xn + emb * gate


def _out_call(ys, x, p, wos, wg, wp, lg, lb, *, tm):
    n, d = x.shape
    row = lambda w: pl.BlockSpec((tm, w), lambda i: (i, 0))
    return pl.pallas_call(
        functools.partial(_out_body, n_y=len(ys)),
        grid=(n // tm,),
        in_specs=([row(y.shape[1]) for y in ys] + [row(d), row(p.shape[1])]
                  + [_const_spec(w.shape) for w in wos]
                  + [_const_spec(wg.shape), _const_spec(wp.shape), _const_spec(lg.shape), _const_spec(lb.shape)]),
        out_specs=row(d),
        out_shape=jax.ShapeDtypeStruct((n, d), F32),
        compiler_params=pltpu.CompilerParams(dimension_semantics=("arbitrary",),
                                             vmem_limit_bytes=VMEM_LIMIT),
        name="out_proj_norm_gate",
    )(*ys, x, p, *wos, wg, wp, lg, lb)


def _odd_in_body(x_ref, w_ref, lbr_ref, q_ref, f_ref, v_ref, zg_ref):
    xt = x_ref[...].astype(BF16)
    lb_raw = lbr_ref[...]
    e = jnp.exp(lb_raw - jnp.max(lb_raw, axis=0, keepdims=True))
    sm = e / jnp.sum(e, axis=0, keepdims=True)
    lower = (sm[0:1, :] + sm[1:2, :]) - sm[0:1, :]
    w = HGRN_WIDTH
    q_ref[...] = _silu(jnp.dot(xt, w_ref[:, 0:w], preferred_element_type=F32)).astype(BF16)
    f_raw = jnp.dot(xt, w_ref[:, w:2 * w], preferred_element_type=F32)
    f_ref[...] = lower + (1.0 - lower) * _sigmoid(f_raw)
    v_ref[...] = jnp.dot(xt, w_ref[:, 2 * w:3 * w], preferred_element_type=F32).astype(BF16)
    zg_ref[...] = _silu(jnp.dot(xt, w_ref[:, 3 * w:4 * w], preferred_element_type=F32)).astype(BF16)


def _odd_in_call(x, w, lbr, *, tm):
    n, d = x.shape
    row = lambda wd: pl.BlockSpec((tm, wd), lambda i: (i, 0))
    return pl.pallas_call(
        _odd_in_body,
        grid=(n // tm,),
        in_specs=[row(d), _const_spec(w.shape), _const_spec(lbr.shape)],
        out_specs=[row(HGRN_WIDTH)] * 4,
        out_shape=[jax.ShapeDtypeStruct((n, HGRN_WIDTH), BF16),
                   jax.ShapeDtypeStruct((n, HGRN_WIDTH), F32),
                   jax.ShapeDtypeStruct((n, HGRN_WIDTH), BF16),
                   jax.ShapeDtypeStruct((n, HGRN_WIDTH), BF16)],
        compiler_params=pltpu.CompilerParams(dimension_semantics=("arbitrary",),
                                             vmem_limit_bytes=VMEM_LIMIT),
        name="odd_in_proj",
    )(x, w, lbr)


def _segment_matrix():
    t = np.arange(CHUNK)[:, None]
    j = np.arange(CHUNK)[None, :]
    blocks = []
    for lb in range(int(math.log2(CHUNK))):
        b = 1 << lb
        start, end = (t // b) * b, (t // b) * b + b - 1
        upper = ((t // b) % 2) == 1
        blocks.append(np.where(upper, (j >= start) & (j <= t), (j > t) & (j <= end)))
    blocks.append(j <= t)
    blocks.append(j > t)
    seg = np.concatenate(blocks, axis=0).astype(np.float32)
    return np.concatenate([seg, seg], axis=1)


def _hgrn_body(q_ref, f_ref, v_ref, zg_ref, seg_ref, gn_ref, o_ref, s_scr, *, tb, hb):
    nchunk = tb // CHUNK
    nlev = int(math.log2(CHUNK))

    @pl.when(pl.program_id(2) == 0)
    def _():
        s_scr[...] = jnp.zeros(s_scr.shape, F32)

    row = lax.broadcasted_iota(jnp.int32, (CHUNK, CHUNK), 0)
    col = lax.broadcasted_iota(jnp.int32, (CHUNK, CHUNK), 1)
    level_masks = [_pair_mask(row, col, lb) for lb in range(nlev)]
    trow = lax.broadcasted_iota(jnp.int32, (CHUNK, LANES), 0)
    upper_rows = [(jnp.right_shift(trow, lb) & 1) == 1 for lb in range(nlev)]
    seg = seg_ref[...]
    gn = gn_ref[...]

    probs = [(h, ci) for h in range(hb) for ci in range(nchunk)]
    pr = []
    for h, ci in probs:
        d = dict(hs=slice(LANES * h, LANES * (h + 1)), rs=slice(CHUNK * ci, CHUNK * (ci + 1)))
        f = f_ref[0, d["rs"], d["hs"]]
        d["q"] = q_ref[0, d["rs"], d["hs"]].astype(F32)
        d["k"] = 1.0 - f
        d["v_b"] = v_ref[0, d["rs"], d["hs"]]
        l2f = jnp.log2(f)
        hi = l2f.astype(BF16)
        lo = (l2f - hi.astype(F32)).astype(BF16)
        d["hilo"] = jnp.concatenate([hi, lo], axis=0)
        pr.append(d)
    for d in pr:
        sums = jnp.dot(seg, d["hilo"], preferred_element_type=F32)
        d["fac"] = jnp.exp2(sums)
    for d in pr:
        d["attn"] = jnp.where(row == col, _mm_nt(d["q"], d["k"]), 0.0)
    for lb in range(nlev):
        for d in pr:
            p = jnp.where(upper_rows[lb], d["q"], d["k"]) * d["fac"][CHUNK * lb:CHUNK * (lb + 1)]
            d["attn"] = jnp.where(level_masks[lb], _mm_nt(p, p), d["attn"])
    for d in pr:
        d["o_intra"] = _mm(d["attn"], d["v_b"])
        inc = d["fac"][CHUNK * nlev:CHUNK * (nlev + 1)]
        exc = d["fac"][CHUNK * (nlev + 1):CHUNK * (nlev + 2)]
        d["q_inc"] = (d["q"] * inc).astype(BF16)
        d["s_decay"] = inc[CHUNK - 1:CHUNK, :]
        d["kv"] = _mm_tn(d["v_b"], d["k"] * exc)

    for h in range(hb):
        state_t = s_scr[h]
        for d, (ph, _) in zip(pr, probs):
            if ph != h:
                continue
            o = d["o_intra"] + _mm_nt(d["q_inc"], state_t)
            state_t = state_t * d["s_decay"] + d["kv"]
            o = o * lax.rsqrt(jnp.mean(o * o, axis=-1, keepdims=True) + NORM_EPS) * gn
            o_ref[0, d["rs"], d["hs"]] = (o * zg_ref[0, d["rs"], d["hs"]].astype(F32)).astype(BF16)
        s_scr[h] = state_t


def _hgrn_call(q, f, v, zg, seg, gn, *, tb, hb):
    bsz, s, w = q.shape
    blk = lambda: pl.BlockSpec((1, tb, hb * LANES), lambda b, g, t: (b, t, g))
    return pl.pallas_call(
        functools.partial(_hgrn_body, tb=tb, hb=hb),
        grid=(bsz, w // (hb * LANES), s // tb),
        in_specs=[blk(), blk(), blk(), blk(), _const_spec(seg.shape), _const_spec(gn.shape)],
        out_specs=blk(),
        out_shape=jax.ShapeDtypeStruct((bsz, s, w), BF16),
        scratch_shapes=[pltpu.VMEM((hb, HEAD_DIM, HEAD_DIM), F32)],
        compiler_params=pltpu.CompilerParams(dimension_semantics=("arbitrary", "arbitrary", "arbitrary"),
                                             vmem_limit_bytes=VMEM_LIMIT),
        name="hgrn2_recurrence",
    )(q, f, v, zg, seg, gn)


def kernel(x, p, w_in_even, conv_a_w, conv_b_w, a_log, dt_bias, gdn_norm_g, w_out_even, w_in_odd,
           lower_bounds, hgrn_norm_g, w_out_odd, ln_g, ln_b, w_pl, w_pl_gate):
    bsz, s, d = x.shape
    n = bsz * s
    wa_end = 4 * CONV_A_WIDTH
    qkv_end = wa_end + 3 * GDN_WIDTH
    zb_end = qkv_end + GDN_WIDTH

    w_in = w_in_even[0].astype(BF16)
    wba = jnp.pad(w_in[:, zb_end:], ((0, 0), (0, LANES - 2 * GDN_HEADS)))
    ya, q, k, v, zg, ba = _even_in_call(
        x, w_in[:, :wa_end], w_in[:, wa_end:qkv_end], w_in[:, qkv_end:zb_end], wba,
        conv_a_w[0], conv_b_w[0], tm=256)

    pad_lane = lambda a: jnp.pad(a, (GDN_HEADS, LANES - 2 * GDN_HEADS))
    arow = jnp.stack([pad_lane(a_log[0]), pad_lane(dt_bias[0])], axis=0)
    acol = jnp.stack([jnp.pad(a_log[0], (GDN_HEADS, 0)), jnp.pad(dt_bias[0], (GDN_HEADS, 0))], axis=1)
    bat = jnp.swapaxes(ba[:, :, :2 * GDN_HEADS], 1, 2)
    og = _gdn_call(q, k, v, zg, ba, bat, arow, acol, gdn_norm_g[0][None, :], tb=256)

    w_out = w_out_even[0].astype(BF16)
    x1 = _out_call([ya.reshape(n, -1), og.reshape(n, -1)], x.reshape(n, d), p[0].reshape(n, PL_DIM),
                   [w_out[:CONV_A_WIDTH], w_out[CONV_A_WIDTH:]], w_pl_gate[0].astype(BF16),
                   w_pl[0].astype(BF16), ln_g[0][None, :], ln_b[0][None, :], tm=256)

    hq, hf, hv, hzg = _odd_in_call(x1, w_in_odd[0].astype(BF16), lower_bounds, tm=256)
    shp = (bsz, s, HGRN_WIDTH)
    seg = jnp.asarray(_segment_matrix(), BF16)
    ho = _hgrn_call(hq.reshape(shp), hf.reshape(shp), hv.reshape(shp), hzg.reshape(shp), seg,
                    hgrn_norm_g[0][None, :], tb=256, hb=4)
    out = _out_call([ho.reshape(n, -1)], x1, p[1].reshape(n, PL_DIM), [w_out_odd[0].astype(BF16)],
                    w_pl_gate[1].astype(BF16), w_pl[1].astype(BF16), ln_g[1][None, :], ln_b[1][None, :],
                    tm=256)
    return out.reshape(bsz, s, d)
```

```python
import functools
import math

import jax
import jax.numpy as jnp
from jax import lax
from jax.experimental import pallas as pl
from jax.experimental.pallas import tpu as pltpu

F32 = jnp.float32
BF16 = jnp.bfloat16

D_MODEL = 1024
DEPTH = 2
PL_DIM = 256
CONV_A_WIDTH = 1024
CONV_A_KERNEL = 3
GDN_HEADS = 8
HEAD_DIM = 128
GDN_WIDTH = GDN_HEADS * HEAD_DIM
GDN_CONV_KERNEL = 4
HGRN_WIDTH = 2 * D_MODEL
HGRN_HEADS = HGRN_WIDTH // HEAD_DIM
DEEPNORM_ALPHA = (2.0 * DEPTH) ** 0.25
NORM_EPS = 1e-5
L2_EPS = 1e-6

LANES = 128
SUBLANES = 8
MXU_COLS = 256
OUT_SUB_ROWS = 128
CHUNK = 64
HALO = SUBLANES
NEG_BIG = -1e30

VMEM_LIMIT = 56 * 1024 * 1024


def _sigmoid(x):
    return 1.0 / (1.0 + jnp.exp(-x))


def _silu(x):
    return x * _sigmoid(x)


def _softplus(x):
    return jnp.maximum(x, 0.0) + jnp.log1p(jnp.exp(-jnp.abs(x)))


def _mm(a, b):
    return jnp.dot(a.astype(BF16), b.astype(BF16), preferred_element_type=F32)


def _mm_nt(a, b):
    return lax.dot_general(a.astype(BF16), b.astype(BF16), (((1,), (1,)), ((), ())),
                           preferred_element_type=F32)


def _mm_tn(a, b):
    return lax.dot_general(a.astype(BF16), b.astype(BF16), (((0,), (0,)), ((), ())),
                           preferred_element_type=F32)


def _mm_f32(a, b):
    return jnp.dot(a, b, preferred_element_type=F32, precision=lax.Precision.HIGHEST)


def _const_spec(shape):
    nd = len(shape)
    return pl.BlockSpec(shape, lambda *_: (0,) * nd, pipeline_mode=pl.Buffered(1))


def _pair_mask(row, col, log2_b):
    return (jnp.right_shift(jnp.bitwise_xor(row, col), log2_b) == 1) & (row > col)


def _even_in_body(x_ref, wa_ref, wqkv_ref, wzb_ref, wba_ref, ca_ref, cb_ref,
                  ya_ref, q_ref, k_ref, v_ref, zg_ref, ba_ref, ua_scr, qkv_scr, *, tm):
    @pl.when(pl.program_id(1) == 0)
    def _():
        ua_scr[0:HALO, :] = jnp.zeros((HALO, CONV_A_WIDTH), F32)
        qkv_scr[0:HALO, :] = jnp.zeros((HALO, 3 * GDN_WIDTH), F32)

    xt = x_ref[0].astype(BF16)
    proj = lambda w_ref, lo: jnp.dot(xt, w_ref[:, lo:lo + MXU_COLS], preferred_element_type=F32)

    def conv_taps(scr, w_ref, cur, sl):
        taps = w_ref.shape[0]
        scr[HALO:HALO + tm, sl] = cur
        ext = scr[0:HALO + tm, sl]
        acc = w_ref[taps - 1:taps, sl] * cur
        for j in range(1, taps):
            acc = acc + w_ref[taps - 1 - j:taps - j, sl] * pltpu.roll(ext, j, axis=0)[HALO:]
        scr[0:HALO, sl] = scr[tm:tm + HALO, sl]
        return acc

    tasks = []

    def mixer_a(g):
        sl = slice(MXU_COLS * g, MXU_COLS * (g + 1))
        def mm():
            return [proj(wa_ref, part * CONV_A_WIDTH + MXU_COLS * g) for part in range(4)]
        def ew(r):
            h, c, b, z = r
            ya_ref[0, :, sl] = (b * conv_taps(ua_scr, ca_ref, c * h, sl) * _silu(z)).astype(BF16)
        return mm, ew

    def mixer_b_in(g):
        sl = slice(MXU_COLS * g, MXU_COLS * (g + 1))
        def mm():
            return proj(wqkv_ref, MXU_COLS * g)
        def ew(r):
            a = _silu(conv_taps(qkv_scr, cb_ref, r, sl))
            for i in range(MXU_COLS // LANES):
                part, hj = divmod(g * (MXU_COLS // LANES) + i, GDN_HEADS)
                hs = slice(LANES * hj, LANES * (hj + 1))
                ah = a[:, LANES * i:LANES * (i + 1)]
                if part < 2:
                    ah = ah * lax.rsqrt(jnp.sum(ah * ah, axis=-1, keepdims=True) + L2_EPS)
                (q_ref, k_ref, v_ref)[part][0, :, hs] = ah.astype(BF16)
        return mm, ew

    def gate_b(g):
        sl = slice(MXU_COLS * g, MXU_COLS * (g + 1))
        def mm():
            return proj(wzb_ref, MXU_COLS * g)
        def ew(r):
            zg_ref[0, :, sl] = _silu(r).astype(BF16)
        return mm, ew

    tasks += [mixer_a(g) for g in range(CONV_A_WIDTH // MXU_COLS)]
    tasks += [mixer_b_in(g) for g in range(3 * GDN_WIDTH // MXU_COLS)]
    tasks += [gate_b(g) for g in range(GDN_WIDTH // MXU_COLS)]
    pending = tasks[0][0]()
    for i, (_, ew) in enumerate(tasks):
        nxt = tasks[i + 1][0]() if i + 1 < len(tasks) else None
        ew(pending)
        pending = nxt
    ba_ref[0] = jnp.dot(xt, wba_ref[...], preferred_element_type=F32)


def _even_in_call(x, wa, wqkv, wzb, wba, ca, cb, *, tm):
    bsz, s, _ = x.shape
    row = lambda w: pl.BlockSpec((1, tm, w), lambda b, t: (b, t, 0))
    return pl.pallas_call(
        functools.partial(_even_in_body, tm=tm),
        grid=(bsz, s // tm),
        in_specs=[row(D_MODEL), _const_spec(wa.shape), _const_spec(wqkv.shape), _const_spec(wzb.shape),
                  _const_spec(wba.shape), _const_spec(ca.shape), _const_spec(cb.shape)],
        out_specs=[row(CONV_A_WIDTH), row(GDN_WIDTH), row(GDN_WIDTH), row(GDN_WIDTH), row(GDN_WIDTH),
                   row(LANES)],
        out_shape=[jax.ShapeDtypeStruct((bsz, s, CONV_A_WIDTH), BF16),
                   jax.ShapeDtypeStruct((bsz, s, GDN_WIDTH), BF16),
                   jax.ShapeDtypeStruct((bsz, s, GDN_WIDTH), BF16),
                   jax.ShapeDtypeStruct((bsz, s, GDN_WIDTH), BF16),
                   jax.ShapeDtypeStruct((bsz, s, GDN_WIDTH), BF16),
                   jax.ShapeDtypeStruct((bsz, s, LANES), F32)],
        scratch_shapes=[pltpu.VMEM((tm + HALO, CONV_A_WIDTH), F32),
                        pltpu.VMEM((tm + HALO, 3 * GDN_WIDTH), F32)],
        compiler_params=pltpu.CompilerParams(dimension_semantics=("arbitrary", "arbitrary"),
                                             vmem_limit_bytes=VMEM_LIMIT),
        name="even_in_proj",
    )(x, wa, wqkv, wzb, wba, ca, cb)


def _gdn_body(q_ref, k_ref, v_ref, zg_ref, ba_ref, bat_ref, arow_ref, acol_ref, gn_ref,
              o_ref, s_scr, *, tb):
    nchunk = tb // CHUNK

    @pl.when(pl.program_id(1) == 0)
    def _():
        s_scr[...] = jnp.zeros(s_scr.shape, F32)

    ba = ba_ref[0]
    beta_cols = _sigmoid(ba)
    g_cols = -jnp.exp(arow_ref[0:1, :]) * _softplus(ba + arow_ref[1:2, :])
    g_rows = -jnp.exp(acol_ref[:, 0:1]) * _softplus(bat_ref[0] + acol_ref[:, 1:2])
    ti = lax.broadcasted_iota(jnp.int32, (tb, tb), 0)
    tj = lax.broadcasted_iota(jnp.int32, (tb, tb), 1)
    same_chunk = (ti // CHUNK) == (tj // CHUNK)
    lower_blk = jnp.where(same_chunk & (tj <= ti), 1.0, 0.0).astype(F32)
    upper_blk = jnp.where(same_chunk & (ti <= tj), 1.0, 0.0).astype(F32)
    gc_cols = _mm_f32(lower_blk, g_cols)
    gc_rows = _mm_f32(g_rows, upper_blk)

    row = lax.broadcasted_iota(jnp.int32, (CHUNK, CHUNK), 0)
    col = lax.broadcasted_iota(jnp.int32, (CHUNK, CHUNK), 1)
    incl = row >= col
    eye = jnp.where(row == col, 1.0, 0.0).astype(F32)
    level_masks = [_pair_mask(row, col, lb) for lb in range(int(math.log2(CHUNK)))]
    scale = HEAD_DIM ** -0.5
    gn = gn_ref[...]

    probs = [(h, ci) for h in range(GDN_HEADS) for ci in range(nchunk)]
    pr = []
    for h, ci in probs:
        hs = slice(LANES * h, LANES * (h + 1))
        rs = slice(CHUNK * ci, CHUNK * (ci + 1))
        d = dict(hs=hs, rs=rs)
        q = q_ref[0, rs, hs].astype(F32) * scale
        d["k_b"] = k_ref[0, rs, hs]
        k = d["k_b"].astype(F32)
        v = v_ref[0, rs, hs].astype(F32)
        beta = beta_cols[rs, h:h + 1]
        gcc = gc_cols[rs, GDN_HEADS + h:GDN_HEADS + h + 1]
        gcr = gc_rows[GDN_HEADS + h:GDN_HEADS + h + 1, rs]
        g_last = gcc[CHUNK - 1:CHUNK, :]
        d["decay"] = jnp.exp(jnp.where(incl, gcc - gcr, NEG_BIG))
        e_gc = jnp.exp(gcc)
        kb = k * beta
        d["kbq"] = jnp.concatenate([kb, q], axis=0).astype(BF16)
        d["rhs"] = jnp.concatenate([v * beta, kb * e_gc], axis=-1).astype(BF16)
        d["q_dec"] = (q * e_gc).astype(BF16)
        d["k_dec"] = (k * jnp.exp(g_last - gcc)).astype(BF16)
        d["s_decay"] = jnp.exp(g_last)
        pr.append(d)

    for d in pr:
        both = _mm_nt(d["kbq"], d["k_b"])
        d["low"] = jnp.where(row > col, both[:CHUNK] * d["decay"], 0.0)
        d["attn"] = (both[CHUNK:] * d["decay"]).astype(BF16)
        d["tinv"] = eye - jnp.where(level_masks[0], d["low"], 0.0)
    for lb in range(1, len(level_masks)):
        for d in pr:
            d["p"] = _mm(d["tinv"], jnp.where(level_masks[lb], d["low"], 0.0))
        for d in pr:
            d["tinv"] = d["tinv"] - _mm(d["p"], d["tinv"])
    for d in pr:
        uw = _mm(d["tinv"], d["rhs"])
        d["u"], d["w"] = uw[:, :HEAD_DIM], uw[:, HEAD_DIM:].astype(BF16)

    states = [s_scr[h] for h in range(GDN_HEADS)]
    for ci in range(nchunk):
        cur = [d for d, (h, c) in zip(pr, probs) if c == ci]
        for h, d in enumerate(cur):
            ws = _mm(jnp.concatenate([d["w"], d["q_dec"]], axis=0), states[h])
            d["v_new"] = (d["u"] - ws[:CHUNK]).astype(BF16)
            d["o_inter"] = ws[CHUNK:]
        for h, d in enumerate(cur):
            o = d["o_inter"] + _mm(d["attn"], d["v_new"])
            states[h] = states[h] * d["s_decay"] + _mm_tn(d["k_dec"], d["v_new"])
            o = o * lax.rsqrt(jnp.mean(o * o, axis=-1, keepdims=True) + NORM_EPS) * gn
            o_ref[0, d["rs"], d["hs"]] = (o * zg_ref[0, d["rs"], d["hs"]].astype(F32)).astype(BF16)
    for h in range(GDN_HEADS):
        s_scr[h] = states[h]


def _gdn_call(q, k, v, zg, ba, bat, arow, acol, gn, *, tb):
    bsz, s, _ = q.shape
    row = lambda w: pl.BlockSpec((1, tb, w), lambda b, t: (b, t, 0))
    return pl.pallas_call(
        functools.partial(_gdn_body, tb=tb),
        grid=(bsz, s // tb),
        in_specs=[row(GDN_WIDTH), row(GDN_WIDTH), row(GDN_WIDTH), row(GDN_WIDTH), row(LANES),
                  pl.BlockSpec((1, 2 * GDN_HEADS, tb), lambda b, t: (b, 0, t)),
                  _const_spec(arow.shape), _const_spec(acol.shape), _const_spec(gn.shape)],
        out_specs=row(GDN_WIDTH),
        out_shape=jax.ShapeDtypeStruct((bsz, s, GDN_WIDTH), BF16),
        scratch_shapes=[pltpu.VMEM((GDN_HEADS, HEAD_DIM, HEAD_DIM), F32)],
        compiler_params=pltpu.CompilerParams(dimension_semantics=("arbitrary", "arbitrary"),
                                             vmem_limit_bytes=VMEM_LIMIT),
        name="gated_delta_rule",
    )(q, k, v, zg, ba, bat, arow, acol, gn)


def _out_body(*refs, n_y):
    y_refs, (x_ref, p_ref) = refs[:n_y], refs[n_y:n_y + 2]
    wo_refs = refs[n_y + 2:2 * n_y + 2]
    wg_ref, wp_ref, lg_ref, lb_ref, o_ref = refs[2 * n_y + 2:]
    tm = x_ref.shape[0]
    subs = [slice(r, r + OUT_SUB_ROWS) for r in range(0, tm, OUT_SUB_ROWS)]
    s_parts, embs, xns, gates = [], [], [], []
    for rs in subs:
        s = jnp.dot(y_refs[0][rs, :], wo_refs[0][...], preferred_element_type=F32)
        for y_ref, wo_ref in zip(y_refs[1:], wo_refs[1:]):
            s = s + jnp.dot(y_ref[rs, :], wo_ref[...], preferred_element_type=F32)
        s_parts.append(s)
    for rs in subs:
        embs.append(jnp.dot(p_ref[rs, :].astype(BF16), wp_ref[...], preferred_element_type=F32))
    for rs, s in zip(subs, s_parts):
        t = DEEPNORM_ALPHA * x_ref[rs, :] + s
        mu = jnp.mean(t, axis=-1, keepdims=True)
        tc = t - mu
        var = jnp.mean(tc * tc, axis=-1, keepdims=True)
        xn = tc * lax.rsqrt(var + NORM_EPS) * lg_ref[...] + lb_ref[...]
        xns.append(xn)
        gates.append(jnp.dot(xn.astype(BF16), wg_ref[...], preferred_element_type=F32))
    for rs, xn, emb, g in zip(subs, xns, embs, gates):
        o_ref[rs, :] = xn + emb * _sigmoid(g)


def _out_call(ys, x, p, wos, wg, wp, lg, lb, *, tm):
    n, d = x.shape
    row = lambda w: pl.BlockSpec((tm, w), lambda i: (i, 0))
    return pl.pallas_call(
        functools.partial(_out_body, n_y=len(ys)),
        grid=(n // tm,),
        in_specs=([row(y.shape[1]) for y in ys] + [row(d), row(p.shape[1])]
                  + [_const_spec(w.shape) for w in wos]
                  + [_const_spec(wg.shape), _const_spec(wp.shape), _const_spec(lg.shape), _const_spec(lb.shape)]),
        out_specs=row(d),
        out_shape=jax.ShapeDtypeStruct((n, d), F32),
        compiler_params=pltpu.CompilerParams(dimension_semantics=("arbitrary",),
                                             vmem_limit_bytes=VMEM_LIMIT),
        name="out_proj_norm_gate",
    )(*ys, x, p, *wos, wg, wp, lg, lb)


def _odd_in_body(x_ref, w_ref, lbr_ref, q_ref, f_ref, v_ref, zg_ref):
    xt = x_ref[...].astype(BF16)
    lb_raw = lbr_ref[...]
    e = jnp.exp(lb_raw - jnp.max(lb_raw, axis=0, keepdims=True))
    sm = e / jnp.sum(e, axis=0, keepdims=True)
    lower = (sm[0:1, :] + sm[1:2, :]) - sm[0:1, :]
    w = HGRN_WIDTH
    q_ref[...] = _silu(jnp.dot(xt, w_ref[:, 0:w], preferred_element_type=F32)).astype(BF16)
    f_raw = jnp.dot(xt, w_ref[:, w:2 * w], preferred_element_type=F32)
    f_ref[...] = lower + (1.0 - lower) * _sigmoid(f_raw)
    v_ref[...] = jnp.dot(xt, w_ref[:, 2 * w:3 * w], preferred_element_type=F32).astype(BF16)
    zg_ref[...] = _silu(jnp.dot(xt, w_ref[:, 3 * w:4 * w], preferred_element_type=F32)).astype(BF16)


def _odd_in_call(x, w, lbr, *, tm):
    n, d = x.shape
    row = lambda wd: pl.BlockSpec((tm, wd), lambda i: (i, 0))
    return pl.pallas_call(
        _odd_in_body,
        grid=(n // tm,),
        in_specs=[row(d), _const_spec(w.shape), _const_spec(lbr.shape)],
        out_specs=[row(HGRN_WIDTH)] * 4,
        out_shape=[jax.ShapeDtypeStruct((n, HGRN_WIDTH), BF16),
                   jax.ShapeDtypeStruct((n, HGRN_WIDTH), F32),
                   jax.ShapeDtypeStruct((n, HGRN_WIDTH), BF16),
                   jax.ShapeDtypeStruct((n, HGRN_WIDTH), BF16)],
        compiler_params=pltpu.CompilerParams(dimension_semantics=("arbitrary",),
                                             vmem_limit_bytes=VMEM_LIMIT),
        name="odd_in_proj",
    )(x, w, lbr)


GROUP = SUBLANES
NGROUP = CHUNK // GROUP


def _group_row_masks(w):
    r = lax.broadcasted_iota(jnp.int32, (NGROUP, GROUP, w), 1)
    return dict(odd=(r & 1) == 1, hi2=(r & 2) != 0, hi4=(r & 4) != 0)


def _decay_products(f, m):
    w = f.shape[-1]
    f3 = f.reshape(NGROUP, GROUP, w)
    odd, hi2, hi4 = m["odd"], m["hi2"], m["hi4"]
    on_rows = lambda x, i: jnp.broadcast_to(x[:, i:i + 1, :], x.shape)
    p2 = f3 * jnp.where(odd, pltpu.roll(f3, 1, axis=1), 1.0)
    p4 = p2 * jnp.where(hi2, jnp.where(hi4, on_rows(p2, 5), on_rows(p2, 1)), 1.0)
    p8 = p4 * jnp.where(hi4, on_rows(p4, 3), 1.0)
    s2 = jnp.where(odd, 1.0, pltpu.roll(f3, GROUP - 1, axis=1))
    s4 = s2 * jnp.where(hi2, 1.0, jnp.where(hi4, on_rows(p2, 7), on_rows(p2, 3)))
    s8 = s4 * jnp.where(hi4, 1.0, on_rows(p4, 7))
    fac = {1: f3, 2: jnp.where(hi2, p2, s2), 4: jnp.where(hi4, p4, s4)}
    fac = {b: e.reshape(CHUNK, w) for b, e in fac.items()}

    tot = [p8[g, GROUP - 1:GROUP, :] for g in range(NGROUP)]

    def prefix_scaled(lo, hi):
        out, run = [p8[lo]], None
        for g in range(lo + 1, hi):
            run = tot[g - 1] if run is None else run * tot[g - 1]
            out.append(p8[g] * run)
        return out, (tot[hi - 1] if run is None else run * tot[hi - 1])

    def suffix_scaled(lo, hi):
        out, run = [s8[hi - 1]], None
        for g in range(hi - 2, lo - 1, -1):
            run = tot[g + 1] if run is None else run * tot[g + 1]
            out.append(s8[g] * run)
        return out[::-1]

    b = GROUP
    while b < CHUNK:
        gpb = b // GROUP
        pieces = []
        for blk in range(NGROUP // gpb):
            lo, hi = blk * gpb, (blk + 1) * gpb
            pieces += prefix_scaled(lo, hi)[0] if blk % 2 else suffix_scaled(lo, hi)
        fac[b] = jnp.concatenate(pieces, axis=0)
        b *= 2
    inc, total = prefix_scaled(0, NGROUP)
    exc = suffix_scaled(0, NGROUP)
    return fac, jnp.concatenate(inc, axis=0), jnp.concatenate(exc, axis=0), total


def _pair_operand(q, k, e, log2_b, upper_rows):
    b = 1 << log2_b
    if b >= GROUP:
        gpb = b // GROUP
        qk = jnp.concatenate([((q if (g // gpb) % 2 else k)[GROUP * g:GROUP * (g + 1)])
                              for g in range(NGROUP)], axis=0)
    elif b == 1:
        return jnp.where(upper_rows, q * e, k).astype(BF16)
    else:
        qk = jnp.where(upper_rows, q, k)
    return (qk * e).astype(BF16)


def _blockdiag2(a):
    z = jnp.zeros((a.shape[0], LANES), a.dtype)
    return jnp.concatenate([jnp.concatenate([a[:, :LANES], z], axis=1),
                            jnp.concatenate([z, a[:, LANES:]], axis=1)], axis=0)


def _hgrn_body(q_ref, f_ref, v_ref, zg_ref, gn_ref, o_ref, s_scr, *, tb, hb):
    nchunk = tb // CHUNK
    nlev = int(math.log2(CHUNK))
    pair_w = 2 * LANES

    @pl.when(pl.program_id(2) == 0)
    def _():
        s_scr[...] = jnp.zeros(s_scr.shape, F32)

    row = lax.broadcasted_iota(jnp.int32, (CHUNK, 2 * CHUNK), 0)
    col = lax.broadcasted_iota(jnp.int32, (CHUNK, 2 * CHUNK), 1) & (CHUNK - 1)
    level_masks = [_pair_mask(row, col, lb) for lb in range(nlev)]
    trow = lax.broadcasted_iota(jnp.int32, (CHUNK, pair_w), 0)
    upper_rows = [(jnp.right_shift(trow, lb) & 1) == 1 for lb in range(int(math.log2(GROUP)))]
    group_masks = _group_row_masks(pair_w)
    gn = gn_ref[...]

    probs = [(hp, ci) for hp in range(hb // 2) for ci in range(nchunk)]
    pr = []
    for hp, ci in probs:
        d = dict(ws=slice(pair_w * hp, pair_w * (hp + 1)), rs=slice(CHUNK * ci, CHUNK * (ci + 1)))
        f = f_ref[0, d["rs"], d["ws"]]
        q = q_ref[0, d["rs"], d["ws"]].astype(F32)
        k = 1.0 - f
        d["v_b"] = v_ref[0, d["rs"], d["ws"]]
        fac, inc, exc, d["s_decay"] = _decay_products(f, group_masks)
        d["q_b"], d["k_b"] = q.astype(BF16), k.astype(BF16)
        d["ops"] = [_pair_operand(q, k, fac[1 << lb], lb, upper_rows[lb] if lb < len(upper_rows) else None)
                    for lb in range(nlev)]
        d["q_inc"] = (q * inc).astype(BF16)
        d["k_exc"] = (k * exc).astype(BF16)
        pr.append(d)
    for d in pr:
        d["attn"] = jnp.where(row == col, _mm_nt(d["q_b"], _blockdiag2(d["k_b"])), 0.0)
    for lb in range(nlev):
        for d in pr:
            p = d["ops"][lb]
            d["attn"] = jnp.where(level_masks[lb], _mm_nt(p, _blockdiag2(p)), d["attn"])
    for d in pr:
        d["o_intra"] = _mm(d["attn"], _blockdiag2(d["v_b"]))
        d["kv"] = [_mm_tn(d["v_b"][:, LANES * i:LANES * (i + 1)], d["k_exc"][:, LANES * i:LANES * (i + 1)])
                   for i in range(2)]

    for hp in range(hb // 2):
        states = [s_scr[2 * hp + i] for i in range(2)]
        for d, (php, _) in zip(pr, probs):
            if php != hp:
                continue
            for i in range(2):
                ls = slice(LANES * i, LANES * (i + 1))
                o = d["o_intra"][:, ls] + _mm_nt(d["q_inc"][:, ls], states[i])
                states[i] = states[i] * d["s_decay"][:, ls] + d["kv"][i]
                o = o * lax.rsqrt(jnp.mean(o * o, axis=-1, keepdims=True) + NORM_EPS) * gn
                os = slice(pair_w * hp + LANES * i, pair_w * hp + LANES * (i + 1))
                o_ref[0, d["rs"], os] = (o * zg_ref[0, d["rs"], os].astype(F32)).astype(BF16)
        for i in range(2):
            s_scr[2 * hp + i] = states[i]


def _hgrn_call(q, f, v, zg, gn, *, tb, hb):
    bsz, s, w = q.shape
    blk = lambda: pl.BlockSpec((1, tb, hb * LANES), lambda b, g, t: (b, t, g))
    return pl.pallas_call(
        functools.partial(_hgrn_body, tb=tb, hb=hb),
        grid=(bsz, w // (hb * LANES), s // tb),
        in_specs=[blk(), blk(), blk(), blk(), _const_spec(gn.shape)],
        out_specs=blk(),
        out_shape=jax.ShapeDtypeStruct((bsz, s, w), BF16),
        scratch_shapes=[pltpu.VMEM((hb, HEAD_DIM, HEAD_DIM), F32)],
        compiler_params=pltpu.CompilerParams(dimension_semantics=("arbitrary", "arbitrary", "arbitrary"),
                                             vmem_limit_bytes=VMEM_LIMIT),
        name="hgrn2_recurrence",
    )(q, f, v, zg, gn)


def kernel(x, p, w_in_even, conv_a_w, conv_b_w, a_log, dt_bias, gdn_norm_g, w_out_even, w_in_odd,
           lower_bounds, hgrn_norm_g, w_out_odd, ln_g, ln_b, w_pl, w_pl_gate):
    bsz, s, d = x.shape
    n = bsz * s
    wa_end = 4 * CONV_A_WIDTH
    qkv_end = wa_end + 3 * GDN_WIDTH
    zb_end = qkv_end + GDN_WIDTH

    w_in = w_in_even[0].astype(BF16)
    wba = jnp.pad(w_in[:, zb_end:], ((0, 0), (0, LANES - 2 * GDN_HEADS)))
    ya, q, k, v, zg, ba = _even_in_call(
        x, w_in[:, :wa_end], w_in[:, wa_end:qkv_end], w_in[:, qkv_end:zb_end], wba,
        conv_a_w[0], conv_b_w[0], tm=256)

    pad_lane = lambda a: jnp.pad(a, (GDN_HEADS, LANES - 2 * GDN_HEADS))
    arow = jnp.stack([pad_lane(a_log[0]), pad_lane(dt_bias[0])], axis=0)
    acol = jnp.stack([jnp.pad(a_log[0], (GDN_HEADS, 0)), jnp.pad(dt_bias[0], (GDN_HEADS, 0))], axis=1)
    bat = jnp.swapaxes(ba[:, :, :2 * GDN_HEADS], 1, 2)
    og = _gdn_call(q, k, v, zg, ba, bat, arow, acol, gdn_norm_g[0][None, :], tb=256)

    w_out = w_out_even[0].astype(BF16)
    x1 = _out_call([ya.reshape(n, -1), og.reshape(n, -1)], x.reshape(n, d), p[0].reshape(n, PL_DIM),
                   [w_out[:CONV_A_WIDTH], w_out[CONV_A_WIDTH:]], w_pl_gate[0].astype(BF16),
                   w_pl[0].astype(BF16), ln_g[0][None, :], ln_b[0][None, :], tm=512)

    hq, hf, hv, hzg = _odd_in_call(x1, w_in_odd[0].astype(BF16), lower_bounds, tm=256)
    shp = (bsz, s, HGRN_WIDTH)
    ho = _hgrn_call(hq.reshape(shp), hf.reshape(shp), hv.reshape(shp), hzg.reshape(shp),
                    hgrn_norm_g[0][None, :], tb=256, hb=4)
    out = _out_call([ho.reshape(n, -1)], x1, p[1].reshape(n, PL_DIM), [w_out_odd[0].astype(BF16)],
                    w_pl_gate[1].astype(BF16), w_pl[1].astype(BF16), ln_g[1][None, :], ln_b[1][None, :],
                    tm=512)
    return out.reshape(bsz, s, d)
```

```python
import functools
import math

import jax
import jax.numpy as jnp
from jax import lax
from jax.experimental import pallas as pl
from jax.experimental.pallas import tpu as pltpu

F32 = jnp.float32
BF16 = jnp.bfloat16

D_MODEL = 1024
DEPTH = 2
PL_DIM = 256
CONV_A_WIDTH = 1024
CONV_A_KERNEL = 3
GDN_HEADS = 8
HEAD_DIM = 128
GDN_WIDTH = GDN_HEADS * HEAD_DIM
GDN_CONV_KERNEL = 4
HGRN_WIDTH = 2 * D_MODEL
HGRN_HEADS = HGRN_WIDTH // HEAD_DIM
DEEPNORM_ALPHA = (2.0 * DEPTH) ** 0.25
NORM_EPS = 1e-5
L2_EPS = 1e-6

LANES = 128
SUBLANES = 8
MXU_COLS = 256
OUT_SUB_ROWS = 128
CHUNK = 64
HALO = SUBLANES
NEG_BIG = -1e30

VMEM_LIMIT = 56 * 1024 * 1024


def _sigmoid(x):
    return 1.0 / (1.0 + jnp.exp(-x))


def _silu(x):
    return x * _sigmoid(x)


def _softplus(x):
    return jnp.maximum(x, 0.0) + jnp.log1p(jnp.exp(-jnp.abs(x)))


def _mm(a, b):
    return jnp.dot(a.astype(BF16), b.astype(BF16), preferred_element_type=F32)


def _mm_nt(a, b):
    return lax.dot_general(a.astype(BF16), b.astype(BF16), (((1,), (1,)), ((), ())),
                           preferred_element_type=F32)


def _mm_tn(a, b):
    return lax.dot_general(a.astype(BF16), b.astype(BF16), (((0,), (0,)), ((), ())),
                           preferred_element_type=F32)


def _mm_f32(a, b):
    return jnp.dot(a, b, preferred_element_type=F32, precision=lax.Precision.HIGHEST)


def _const_spec(shape):
    nd = len(shape)
    return pl.BlockSpec(shape, lambda *_: (0,) * nd, pipeline_mode=pl.Buffered(1))


def _pair_mask(row, col, log2_b):
    return (jnp.right_shift(jnp.bitwise_xor(row, col), log2_b) == 1) & (row > col)


def _even_in_body(x_ref, wa_ref, wqkv_ref, wzb_ref, wba_ref, ca_ref, cb_ref,
                  ya_ref, q_ref, k_ref, v_ref, zg_ref, ba_ref, ua_scr, qkv_scr, *, tm):
    @pl.when(pl.program_id(1) == 0)
    def _():
        ua_scr[0:HALO, :] = jnp.zeros((HALO, CONV_A_WIDTH), F32)
        qkv_scr[0:HALO, :] = jnp.zeros((HALO, 3 * GDN_WIDTH), F32)

    xt = x_ref[0].astype(BF16)
    proj = lambda w_ref, lo: jnp.dot(xt, w_ref[:, lo:lo + MXU_COLS], preferred_element_type=F32)

    def conv_taps(scr, w_ref, cur, sl):
        taps = w_ref.shape[0]
        scr[HALO:HALO + tm, sl] = cur
        ext = scr[0:HALO + tm, sl]
        acc = w_ref[taps - 1:taps, sl] * cur
        for j in range(1, taps):
            acc = acc + w_ref[taps - 1 - j:taps - j, sl] * pltpu.roll(ext, j, axis=0)[HALO:]
        scr[0:HALO, sl] = scr[tm:tm + HALO, sl]
        return acc

    tasks = []

    def mixer_a(g):
        sl = slice(MXU_COLS * g, MXU_COLS * (g + 1))
        def mm():
            return [proj(wa_ref, part * CONV_A_WIDTH + MXU_COLS * g) for part in range(4)]
        def ew(r):
            h, c, b, z = r
            ya_ref[0, :, sl] = (b * conv_taps(ua_scr, ca_ref, c * h, sl) * _silu(z)).astype(BF16)
        return mm, ew

    def mixer_b_in(g):
        sl = slice(MXU_COLS * g, MXU_COLS * (g + 1))
        def mm():
            return proj(wqkv_ref, MXU_COLS * g)
        def ew(r):
            a = _silu(conv_taps(qkv_scr, cb_ref, r, sl))
            for i in range(MXU_COLS // LANES):
                part, hj = divmod(g * (MXU_COLS // LANES) + i, GDN_HEADS)
                hs = slice(LANES * hj, LANES * (hj + 1))
                ah = a[:, LANES * i:LANES * (i + 1)]
                if part < 2:
                    ah = ah * lax.rsqrt(jnp.sum(ah * ah, axis=-1, keepdims=True) + L2_EPS)
                (q_ref, k_ref, v_ref)[part][0, :, hs] = ah.astype(BF16)
        return mm, ew

    def gate_b(g):
        sl = slice(MXU_COLS * g, MXU_COLS * (g + 1))
        def mm():
            return proj(wzb_ref, MXU_COLS * g)
        def ew(r):
            zg_ref[0, :, sl] = _silu(r).astype(BF16)
        return mm, ew

    n_a, n_b = CONV_A_WIDTH // MXU_COLS, 3 * GDN_WIDTH // MXU_COLS
    for i in range(n_a):
        tasks.append(mixer_a(i))
        tasks += [mixer_b_in(g) for g in range(i * n_b // n_a, (i + 1) * n_b // n_a)]
        tasks.append(gate_b(i))
    for mm, ew in tasks:
        ew(mm())
    ba_ref[0] = jnp.dot(xt, wba_ref[...], preferred_element_type=F32)


def _even_in_call(x, wa, wqkv, wzb, wba, ca, cb, *, tm):
    bsz, s, _ = x.shape
    row = lambda w: pl.BlockSpec((1, tm, w), lambda b, t: (b, t, 0))
    return pl.pallas_call(
        functools.partial(_even_in_body, tm=tm),
        grid=(bsz, s // tm),
        in_specs=[row(D_MODEL), _const_spec(wa.shape), _const_spec(wqkv.shape), _const_spec(wzb.shape),
                  _const_spec(wba.shape), _const_spec(ca.shape), _const_spec(cb.shape)],
        out_specs=[row(CONV_A_WIDTH), row(GDN_WIDTH), row(GDN_WIDTH), row(GDN_WIDTH), row(GDN_WIDTH),
                   row(LANES)],
        out_shape=[jax.ShapeDtypeStruct((bsz, s, CONV_A_WIDTH), BF16),
                   jax.ShapeDtypeStruct((bsz, s, GDN_WIDTH), BF16),
                   jax.ShapeDtypeStruct((bsz, s, GDN_WIDTH), BF16),
                   jax.ShapeDtypeStruct((bsz, s, GDN_WIDTH), BF16),
                   jax.ShapeDtypeStruct((bsz, s, GDN_WIDTH), BF16),
                   jax.ShapeDtypeStruct((bsz, s, LANES), F32)],
        scratch_shapes=[pltpu.VMEM((tm + HALO, CONV_A_WIDTH), F32),
                        pltpu.VMEM((tm + HALO, 3 * GDN_WIDTH), F32)],
        compiler_params=pltpu.CompilerParams(dimension_semantics=("arbitrary", "arbitrary"),
                                             vmem_limit_bytes=VMEM_LIMIT),
        name="even_in_proj",
    )(x, wa, wqkv, wzb, wba, ca, cb)


def _gdn_body(q_ref, k_ref, v_ref, zg_ref, ba_ref, arow_ref, acol_ref, gn_ref,
              o_ref, s_scr, *, tb):
    nchunk = tb // CHUNK

    @pl.when(pl.program_id(1) == 0)
    def _():
        s_scr[...] = jnp.zeros(s_scr.shape, F32)

    ba = ba_ref[0]
    beta_cols = _sigmoid(ba)
    g_cols = -jnp.exp(arow_ref[0:1, :]) * _softplus(ba + arow_ref[1:2, :])
    ba_rows = jnp.transpose(ba)[0:2 * GDN_HEADS, :]
    g_rows = -jnp.exp(acol_ref[:, 0:1]) * _softplus(ba_rows + acol_ref[:, 1:2])
    ti = lax.broadcasted_iota(jnp.int32, (tb, tb), 0)
    tj = lax.broadcasted_iota(jnp.int32, (tb, tb), 1)
    same_chunk = (ti // CHUNK) == (tj // CHUNK)
    lower_blk = jnp.where(same_chunk & (tj <= ti), 1.0, 0.0).astype(F32)
    upper_blk = jnp.where(same_chunk & (ti <= tj), 1.0, 0.0).astype(F32)
    gc_cols = _mm_f32(lower_blk, g_cols)
    gc_rows = _mm_f32(g_rows, upper_blk)

    pair_w = 2 * LANES
    row = lax.broadcasted_iota(jnp.int32, (CHUNK, 2 * CHUNK), 0)
    lane = lax.broadcasted_iota(jnp.int32, (CHUNK, 2 * CHUNK), 1)
    col = lane & (CHUNK - 1)
    first = lane < CHUNK
    eye = jnp.where(row == col, 1.0, 0.0).astype(F32)
    level_masks = [_pair_mask(row, col, lb) for lb in range(int(math.log2(CHUNK)))]
    lane1 = lax.broadcasted_iota(jnp.int32, (1, LANES), 1)
    scale = HEAD_DIM ** -0.5
    gn = gn_ref[...]

    def blockdiag_c(a, keep=None):
        top = first if keep is None else keep & first
        bot = ~first if keep is None else keep & ~first
        return jnp.concatenate([jnp.where(top, a, 0.0), jnp.where(bot, a, 0.0)], axis=0).astype(BF16)

    def per_head(c0, c1):
        return jnp.concatenate([jnp.broadcast_to(c0, (CHUNK, LANES)), jnp.broadcast_to(c1, (CHUNK, LANES))],
                               axis=1)

    probs = [(hp, ci) for hp in range(GDN_HEADS // 2) for ci in range(nchunk)]
    pr = []
    for hp, ci in probs:
        h0, h1 = 2 * hp, 2 * hp + 1
        rs = slice(CHUNK * ci, CHUNK * (ci + 1))
        ws = slice(pair_w * hp, pair_w * (hp + 1))
        d = dict(hp=hp, rs=rs, ws=ws)
        q = q_ref[0, rs, ws].astype(F32) * scale
        d["k_b"] = k_ref[0, rs, ws]
        k = d["k_b"].astype(F32)
        v = v_ref[0, rs, ws].astype(F32)
        gcc = [gc_cols[rs, GDN_HEADS + h:GDN_HEADS + h + 1] for h in (h0, h1)]
        g_last = [g[CHUNK - 1:CHUNK, :] for g in gcc]
        tile = slice(LANES * (ci // 2), LANES * (ci // 2 + 1))
        r0 = gc_rows[GDN_HEADS + h0:GDN_HEADS + h0 + 1, tile]
        r1 = gc_rows[GDN_HEADS + h1:GDN_HEADS + h1 + 1, tile]
        if ci % 2:
            r0 = pltpu.roll(r0, CHUNK, axis=1)
        else:
            r1 = pltpu.roll(r1, CHUNK, axis=1)
        gcr = jnp.where(lane1 < CHUNK, r0, r1)
        gcc2 = jnp.where(first, gcc[0], gcc[1])
        d["decay"] = jnp.exp(jnp.where(row >= col, gcc2 - gcr, NEG_BIG))
        beta = per_head(beta_cols[rs, h0:h0 + 1], beta_cols[rs, h1:h1 + 1])
        e_gc = per_head(jnp.exp(gcc[0]), jnp.exp(gcc[1]))
        kb = k * beta
        d["kbq"] = jnp.concatenate([kb, q], axis=0).astype(BF16)
        vb, kbe = (v * beta).astype(BF16), (kb * e_gc).astype(BF16)
        z = jnp.zeros((CHUNK, pair_w), BF16)
        d["rhs"] = jnp.concatenate(
            [jnp.concatenate([vb[:, :LANES], kbe[:, :LANES], z], axis=1),
             jnp.concatenate([z, vb[:, LANES:], kbe[:, LANES:]], axis=1)], axis=0)
        d["q_dec"] = (q * e_gc).astype(BF16)
        d["k_dec"] = (k * per_head(jnp.exp(g_last[0] - gcc[0]), jnp.exp(g_last[1] - gcc[1]))).astype(BF16)
        d["s_decay"] = [jnp.exp(g) for g in g_last]
        pr.append(d)

    for d in pr:
        both = _mm_nt(d["kbq"], _blockdiag2(d["k_b"]))
        d["low"] = jnp.where(row > col, both[:CHUNK] * d["decay"], 0.0)
        d["attn"] = (both[CHUNK:] * d["decay"]).astype(BF16)
        d["tinv"] = eye - jnp.where(level_masks[0], d["low"], 0.0)
    for lb in range(1, len(level_masks)):
        for d in pr:
            d["p"] = _mm(d["tinv"], blockdiag_c(d["low"], level_masks[lb]))
        for d in pr:
            d["tinv"] = d["tinv"] - _mm(d["p"], blockdiag_c(d["tinv"]))
    for d in pr:
        d["uw"] = _mm(d["tinv"], d["rhs"])

    states = [s_scr[h] for h in range(GDN_HEADS)]
    for ci in range(nchunk):
        cur = [d for d, (_, c) in zip(pr, probs) if c == ci]
        for d in cur:
            d["wq_s"] = []
            for i in range(2):
                ls = slice(LANES * i, LANES * (i + 1))
                w = d["uw"][:, pair_w * i + LANES:pair_w * (i + 1)].astype(BF16)
                d["wq_s"].append(_mm(jnp.concatenate([w, d["q_dec"][:, ls]], axis=0), states[2 * d["hp"] + i]))
        for d in cur:
            v_new = jnp.concatenate([d["uw"][:, pair_w * i:pair_w * i + LANES] - d["wq_s"][i][:CHUNK]
                                     for i in range(2)], axis=1).astype(BF16)
            o_pair = (jnp.concatenate([d["wq_s"][i][CHUNK:] for i in range(2)], axis=1)
                      + _mm(d["attn"], _blockdiag2(v_new)))
            for i in range(2):
                h = 2 * d["hp"] + i
                ls = slice(LANES * i, LANES * (i + 1))
                hs = slice(LANES * h, LANES * (h + 1))
                states[h] = states[h] * d["s_decay"][i] + _mm_tn(d["k_dec"][:, ls], v_new[:, ls])
                o = o_pair[:, ls]
                o = o * lax.rsqrt(jnp.mean(o * o, axis=-1, keepdims=True) + NORM_EPS) * gn
                o_ref[0, d["rs"], hs] = (o * zg_ref[0, d["rs"], hs].astype(F32)).astype(BF16)
    for h in range(GDN_HEADS):
        s_scr[h] = states[h]


def _gdn_call(q, k, v, zg, ba, arow, acol, gn, *, tb):
    bsz, s, _ = q.shape
    row = lambda w: pl.BlockSpec((1, tb, w), lambda b, t: (b, t, 0))
    return pl.pallas_call(
        functools.partial(_gdn_body, tb=tb),
        grid=(bsz, s // tb),
        in_specs=[row(GDN_WIDTH), row(GDN_WIDTH), row(GDN_WIDTH), row(GDN_WIDTH), row(LANES),
                  _const_spec(arow.shape), _const_spec(acol.shape), _const_spec(gn.shape)],
        out_specs=row(GDN_WIDTH),
        out_shape=jax.ShapeDtypeStruct((bsz, s, GDN_WIDTH), BF16),
        scratch_shapes=[pltpu.VMEM((GDN_HEADS, HEAD_DIM, HEAD_DIM), F32)],
        compiler_params=pltpu.CompilerParams(dimension_semantics=("arbitrary", "arbitrary"),
                                             vmem_limit_bytes=VMEM_LIMIT),
        name="gated_delta_rule",
    )(q, k, v, zg, ba, arow, acol, gn)


def _out_body(*refs, n_y):
    y_refs, (x_ref, p_ref) = refs[:n_y], refs[n_y:n_y + 2]
    wo_refs = refs[n_y + 2:2 * n_y + 2]
    wg_ref, wp_ref, lg_ref, lb_ref, o_ref = refs[2 * n_y + 2:]
    tm = x_ref.shape[0]
    subs = [slice(r, r + OUT_SUB_ROWS) for r in range(0, tm, OUT_SUB_ROWS)]
    s_parts, embs, xns, gates = [], [], [], []
    for rs in subs:
        s = jnp.dot(y_refs[0][rs, :], wo_refs[0][...], preferred_element_type=F32)
        for y_ref, wo_ref in zip(y_refs[1:], wo_refs[1:]):
            s = s + jnp.dot(y_ref[rs, :], wo_ref[...], preferred_element_type=F32)
        s_parts.append(s)
    for rs in subs:
        embs.append(jnp.dot(p_ref[rs, :].astype(BF16), wp_ref[...], preferred_element_type=F32))
    for rs, s in zip(subs, s_parts):
        t = DEEPNORM_ALPHA * x_ref[rs, :] + s
        mu = jnp.mean(t, axis=-1, keepdims=True)
        tc = t - mu
        var = jnp.mean(tc * tc, axis=-1, keepdims=True)
        xn = tc * lax.rsqrt(var + NORM_EPS) * lg_ref[...] + lb_ref[...]
        xns.append(xn)
        gates.append(jnp.dot(xn.astype(BF16), wg_ref[...], preferred_element_type=F32))
    for rs, xn, emb, g in zip(subs, xns, embs, gates):
        o_ref[rs, :] = xn + emb * _sigmoid(g)


def _out_call(ys, x, p, wos, wg, wp, lg, lb, *, tm):
    n, d = x.shape
    row = lambda w: pl.BlockSpec((tm, w), lambda i: (i, 0))
    return pl.pallas_call(
        functools.partial(_out_body, n_y=len(ys)),
        grid=(n // tm,),
        in_specs=([row(y.shape[1]) for y in ys] + [row(d), row(p.shape[1])]
                  + [_const_spec(w.shape) for w in wos]
                  + [_const_spec(wg.shape), _const_spec(wp.shape), _const_spec(lg.shape), _const_spec(lb.shape)]),
        out_specs=row(d),
        out_shape=jax.ShapeDtypeStruct((n, d), F32),
        compiler_params=pltpu.CompilerParams(dimension_semantics=("arbitrary",),
                                             vmem_limit_bytes=VMEM_LIMIT),
        name="out_proj_norm_gate",
    )(*ys, x, p, *wos, wg, wp, lg, lb)


def _odd_in_body(x_ref, w_ref, lbr_ref, q_ref, f_ref, v_ref, zg_ref):
    xt = x_ref[...].astype(BF16)
    lb_raw = lbr_ref[...]
    e = jnp.exp(lb_raw - jnp.max(lb_raw, axis=0, keepdims=True))
    sm = e / jnp.sum(e, axis=0, keepdims=True)
    lower = (sm[0:1, :] + sm[1:2, :]) - sm[0:1, :]
    w = HGRN_WIDTH
    q_ref[...] = _silu(jnp.dot(xt, w_ref[:, 0:w], preferred_element_type=F32)).astype(BF16)
    f_raw = jnp.dot(xt, w_ref[:, w:2 * w], preferred_element_type=F32)
    f_ref[...] = lower + (1.0 - lower) * _sigmoid(f_raw)
    v_ref[...] = jnp.dot(xt, w_ref[:, 2 * w:3 * w], preferred_element_type=F32).astype(BF16)
    zg_ref[...] = _silu(jnp.dot(xt, w_ref[:, 3 * w:4 * w], preferred_element_type=F32)).astype(BF16)


def _odd_in_call(x, w, lbr, *, tm):
    n, d = x.shape
    row = lambda wd: pl.BlockSpec((tm, wd), lambda i: (i, 0))
    return pl.pallas_call(
        _odd_in_body,
        grid=(n // tm,),
        in_specs=[row(d), _const_spec(w.shape), _const_spec(lbr.shape)],
        out_specs=[row(HGRN_WIDTH)] * 4,
        out_shape=[jax.ShapeDtypeStruct((n, HGRN_WIDTH), BF16),
                   jax.ShapeDtypeStruct((n, HGRN_WIDTH), F32),
                   jax.ShapeDtypeStruct((n, HGRN_WIDTH), BF16),
                   jax.ShapeDtypeStruct((n, HGRN_WIDTH), BF16)],
        compiler_params=pltpu.CompilerParams(dimension_semantics=("arbitrary",),
                                             vmem_limit_bytes=VMEM_LIMIT),
        name="odd_in_proj",
    )(x, w, lbr)


GROUP = SUBLANES
NGROUP = CHUNK // GROUP


def _group_row_masks(w):
    r = lax.broadcasted_iota(jnp.int32, (NGROUP, GROUP, w), 1)
    return dict(odd=(r & 1) == 1, hi2=(r & 2) != 0, hi4=(r & 4) != 0)


def _decay_products(f, m):
    w = f.shape[-1]
    f3 = f.reshape(NGROUP, GROUP, w)
    odd, hi2, hi4 = m["odd"], m["hi2"], m["hi4"]
    on_rows = lambda x, i: jnp.broadcast_to(x[:, i:i + 1, :], x.shape)
    p2 = f3 * jnp.where(odd, pltpu.roll(f3, 1, axis=1), 1.0)
    p4 = p2 * jnp.where(hi2, jnp.where(hi4, on_rows(p2, 5), on_rows(p2, 1)), 1.0)
    p8 = p4 * jnp.where(hi4, on_rows(p4, 3), 1.0)
    s2 = jnp.where(odd, 1.0, pltpu.roll(f3, GROUP - 1, axis=1))
    s4 = s2 * jnp.where(hi2, 1.0, jnp.where(hi4, on_rows(p2, 7), on_rows(p2, 3)))
    s8 = s4 * jnp.where(hi4, 1.0, on_rows(p4, 7))
    fac = {1: f3, 2: jnp.where(hi2, p2, s2), 4: jnp.where(hi4, p4, s4)}
    fac = {b: e.reshape(CHUNK, w) for b, e in fac.items()}

    tot = [p8[g, GROUP - 1:GROUP, :] for g in range(NGROUP)]

    def prefix_scaled(lo, hi):
        out, run = [p8[lo]], None
        for g in range(lo + 1, hi):
            run = tot[g - 1] if run is None else run * tot[g - 1]
            out.append(p8[g] * run)
        return out, (tot[hi - 1] if run is None else run * tot[hi - 1])

    def suffix_scaled(lo, hi):
        out, run = [s8[hi - 1]], None
        for g in range(hi - 2, lo - 1, -1):
            run = tot[g + 1] if run is None else run * tot[g + 1]
            out.append(s8[g] * run)
        return out[::-1]

    b = GROUP
    while b < CHUNK:
        gpb = b // GROUP
        pieces = []
        for blk in range(NGROUP // gpb):
            lo, hi = blk * gpb, (blk + 1) * gpb
            pieces += prefix_scaled(lo, hi)[0] if blk % 2 else suffix_scaled(lo, hi)
        fac[b] = jnp.concatenate(pieces, axis=0)
        b *= 2
    inc, total = prefix_scaled(0, NGROUP)
    exc = suffix_scaled(0, NGROUP)
    return fac, jnp.concatenate(inc, axis=0), jnp.concatenate(exc, axis=0), total


def _pair_operand(q, k, e, log2_b, upper_rows):
    b = 1 << log2_b
    if b >= GROUP:
        gpb = b // GROUP
        qk = jnp.concatenate([((q if (g // gpb) % 2 else k)[GROUP * g:GROUP * (g + 1)])
                              for g in range(NGROUP)], axis=0)
    elif b == 1:
        return jnp.where(upper_rows, q * e, k).astype(BF16)
    else:
        qk = jnp.where(upper_rows, q, k)
    return (qk * e).astype(BF16)


def _blockdiag2(a):
    z = jnp.zeros((a.shape[0], LANES), a.dtype)
    return jnp.concatenate([jnp.concatenate([a[:, :LANES], z], axis=1),
                            jnp.concatenate([z, a[:, LANES:]], axis=1)], axis=0)


def _hgrn_body(q_ref, f_ref, v_ref, zg_ref, gn_ref, o_ref, s_scr, *, tb, hb):
    nchunk = tb // CHUNK
    nlev = int(math.log2(CHUNK))
    pair_w = 2 * LANES

    @pl.when(pl.program_id(2) == 0)
    def _():
        s_scr[...] = jnp.zeros(s_scr.shape, F32)

    row = lax.broadcasted_iota(jnp.int32, (CHUNK, 2 * CHUNK), 0)
    col = lax.broadcasted_iota(jnp.int32, (CHUNK, 2 * CHUNK), 1) & (CHUNK - 1)
    level_masks = [_pair_mask(row, col, lb) for lb in range(nlev)]
    trow = lax.broadcasted_iota(jnp.int32, (CHUNK, pair_w), 0)
    upper_rows = [(jnp.right_shift(trow, lb) & 1) == 1 for lb in range(int(math.log2(GROUP)))]
    group_masks = _group_row_masks(pair_w)
    gn = gn_ref[...]

    probs = [(hp, ci) for hp in range(hb // 2) for ci in range(nchunk)]
    pr = []
    for hp, ci in probs:
        d = dict(ws=slice(pair_w * hp, pair_w * (hp + 1)), rs=slice(CHUNK * ci, CHUNK * (ci + 1)))
        f = f_ref[0, d["rs"], d["ws"]]
        q = q_ref[0, d["rs"], d["ws"]].astype(F32)
        k = 1.0 - f
        d["v_b"] = v_ref[0, d["rs"], d["ws"]]
        fac, inc, exc, d["s_decay"] = _decay_products(f, group_masks)
        d["q_b"], d["k_b"] = q.astype(BF16), k.astype(BF16)
        d["ops"] = [_pair_operand(q, k, fac[1 << lb], lb, upper_rows[lb] if lb < len(upper_rows) else None)
                    for lb in range(nlev)]
        d["q_inc"] = (q * inc).astype(BF16)
        d["k_exc"] = (k * exc).astype(BF16)
        pr.append(d)
    for d in pr:
        d["attn"] = jnp.where(row == col, _mm_nt(d["q_b"], _blockdiag2(d["k_b"])), 0.0)
    for lb in range(nlev):
        for d in pr:
            p = d["ops"][lb]
            d["attn"] = jnp.where(level_masks[lb], _mm_nt(p, _blockdiag2(p)), d["attn"])
    for d in pr:
        d["o_intra"] = _mm(d["attn"], _blockdiag2(d["v_b"]))
        d["kv"] = [_mm_tn(d["v_b"][:, LANES * i:LANES * (i + 1)], d["k_exc"][:, LANES * i:LANES * (i + 1)])
                   for i in range(2)]

    for hp in range(hb // 2):
        states = [s_scr[2 * hp + i] for i in range(2)]
        for d, (php, _) in zip(pr, probs):
            if php != hp:
                continue
            for i in range(2):
                ls = slice(LANES * i, LANES * (i + 1))
                o = d["o_intra"][:, ls] + _mm_nt(d["q_inc"][:, ls], states[i])
                states[i] = states[i] * d["s_decay"][:, ls] + d["kv"][i]
                o = o * lax.rsqrt(jnp.mean(o * o, axis=-1, keepdims=True) + NORM_EPS) * gn
                os = slice(pair_w * hp + LANES * i, pair_w * hp + LANES * (i + 1))
                o_ref[0, d["rs"], os] = (o * zg_ref[0, d["rs"], os].astype(F32)).astype(BF16)
        for i in range(2):
            s_scr[2 * hp + i] = states[i]


def _hgrn_call(q, f, v, zg, gn, *, tb, hb):
    bsz, s, w = q.shape
    blk = lambda: pl.BlockSpec((1, tb, hb * LANES), lambda b, g, t: (b, t, g))
    return pl.pallas_call(
        functools.partial(_hgrn_body, tb=tb, hb=hb),
        grid=(bsz, w // (hb * LANES), s // tb),
        in_specs=[blk(), blk(), blk(), blk(), _const_spec(gn.shape)],
        out_specs=blk(),
        out_shape=jax.ShapeDtypeStruct((bsz, s, w), BF16),
        scratch_shapes=[pltpu.VMEM((hb, HEAD_DIM, HEAD_DIM), F32)],
        compiler_params=pltpu.CompilerParams(dimension_semantics=("arbitrary", "arbitrary", "arbitrary"),
                                             vmem_limit_bytes=VMEM_LIMIT),
        name="hgrn2_recurrence",
    )(q, f, v, zg, gn)


def kernel(x, p, w_in_even, conv_a_w, conv_b_w, a_log, dt_bias, gdn_norm_g, w_out_even, w_in_odd,
           lower_bounds, hgrn_norm_g, w_out_odd, ln_g, ln_b, w_pl, w_pl_gate):
    bsz, s, d = x.shape
    n = bsz * s
    wa_end = 4 * CONV_A_WIDTH
    qkv_end = wa_end + 3 * GDN_WIDTH
    zb_end = qkv_end + GDN_WIDTH

    w_in = w_in_even[0].astype(BF16)
    wba = jnp.pad(w_in[:, zb_end:], ((0, 0), (0, LANES - 2 * GDN_HEADS)))
    ya, q, k, v, zg, ba = _even_in_call(
        x, w_in[:, :wa_end], w_in[:, wa_end:qkv_end], w_in[:, qkv_end:zb_end], wba,
        conv_a_w[0], conv_b_w[0], tm=256)

    pad_lane = lambda a: jnp.pad(a, (GDN_HEADS, LANES - 2 * GDN_HEADS))
    arow = jnp.stack([pad_lane(a_log[0]), pad_lane(dt_bias[0])], axis=0)
    acol = jnp.stack([jnp.pad(a_log[0], (GDN_HEADS, 0)), jnp.pad(dt_bias[0], (GDN_HEADS, 0))], axis=1)
    og = _gdn_call(q, k, v, zg, ba, arow, acol, gdn_norm_g[0][None, :], tb=256)

    w_out = w_out_even[0].astype(BF16)
    x1 = _out_call([ya.reshape(n, -1), og.reshape(n, -1)], x.reshape(n, d), p[0].reshape(n, PL_DIM),
                   [w_out[:CONV_A_WIDTH], w_out[CONV_A_WIDTH:]], w_pl_gate[0].astype(BF16),
                   w_pl[0].astype(BF16), ln_g[0][None, :], ln_b[0][None, :], tm=512)

    hq, hf, hv, hzg = _odd_in_call(x1, w_in_odd[0].astype(BF16), lower_bounds, tm=256)
    shp = (bsz, s, HGRN_WIDTH)
    ho = _hgrn_call(hq.reshape(shp), hf.reshape(shp), hv.reshape(shp), hzg.reshape(shp),
                    hgrn_norm_g[0][None, :], tb=512, hb=8)
    out = _out_call([ho.reshape(n, -1)], x1, p[1].reshape(n, PL_DIM), [w_out_odd[0].astype(BF16)],
                    w_pl_gate[1].astype(BF16), w_pl[1].astype(BF16), ln_g[1][None, :], ln_b[1][None, :],
                    tm=512)
    return out.reshape(bsz, s, d)
```

```python
import functools
import math

import jax
import jax.numpy as jnp
from jax import lax
from jax.experimental import pallas as pl
from jax.experimental.pallas import tpu as pltpu

F32 = jnp.float32
BF16 = jnp.bfloat16

D_MODEL = 1024
DEPTH = 2
PL_DIM = 256
CONV_A_WIDTH = 1024
CONV_A_KERNEL = 3
GDN_HEADS = 8
HEAD_DIM = 128
GDN_WIDTH = GDN_HEADS * HEAD_DIM
GDN_CONV_KERNEL = 4
HGRN_WIDTH = 2 * D_MODEL
HGRN_HEADS = HGRN_WIDTH // HEAD_DIM
DEEPNORM_ALPHA = (2.0 * DEPTH) ** 0.25
NORM_EPS = 1e-5
L2_EPS = 1e-6

LANES = 128
SUBLANES = 8
MXU_COLS = 256
OUT_SUB_ROWS = 128
CHUNK = 64
HALO = SUBLANES
NEG_BIG = -1e30

VMEM_LIMIT = 56 * 1024 * 1024


def _sigmoid(x):
    return 1.0 / (1.0 + jnp.exp(-x))


def _silu(x):
    return x * _sigmoid(x)


def _softplus(x):
    return jnp.maximum(x, 0.0) + jnp.log1p(jnp.exp(-jnp.abs(x)))


def _mm(a, b):
    return jnp.dot(a.astype(BF16), b.astype(BF16), preferred_element_type=F32)


def _mm_nt(a, b):
    return lax.dot_general(a.astype(BF16), b.astype(BF16), (((1,), (1,)), ((), ())),
                           preferred_element_type=F32)


def _mm_tn(a, b):
    return lax.dot_general(a.astype(BF16), b.astype(BF16), (((0,), (0,)), ((), ())),
                           preferred_element_type=F32)


def _mm_f32(a, b):
    return jnp.dot(a, b, preferred_element_type=F32, precision=lax.Precision.HIGHEST)


def _const_spec(shape):
    nd = len(shape)
    return pl.BlockSpec(shape, lambda *_: (0,) * nd, pipeline_mode=pl.Buffered(1))


def _pair_mask(row, col, log2_b):
    return (jnp.right_shift(jnp.bitwise_xor(row, col), log2_b) == 1) & (row > col)


def _even_in_body(x_ref, w_ref, wba_ref, ca_ref, cb_ref,
                  ya_ref, q_ref, k_ref, v_ref, zg_ref, ba_ref, ua_scr, qkv_scr, *, tm):
    @pl.when(pl.program_id(1) == 0)
    def _():
        ua_scr[0:HALO, :] = jnp.zeros((HALO, CONV_A_WIDTH), F32)
        qkv_scr[0:HALO, :] = jnp.zeros((HALO, 3 * GDN_WIDTH), F32)

    xt = x_ref[0].astype(BF16)
    proj = lambda lo: jnp.dot(xt, w_ref[:, lo:lo + MXU_COLS], preferred_element_type=F32)
    qkv_lo, zb_lo = 4 * CONV_A_WIDTH, 4 * CONV_A_WIDTH + 3 * GDN_WIDTH

    def conv_taps(scr, w_ref, cur, sl):
        taps = w_ref.shape[0]
        scr[HALO:HALO + tm, sl] = cur
        ext = scr[0:HALO + tm, sl]
        acc = w_ref[taps - 1:taps, sl] * cur
        for j in range(1, taps):
            acc = acc + w_ref[taps - 1 - j:taps - j, sl] * pltpu.roll(ext, j, axis=0)[HALO:]
        scr[0:HALO, sl] = scr[tm:tm + HALO, sl]
        return acc

    tasks = []

    def mixer_a(g):
        sl = slice(MXU_COLS * g, MXU_COLS * (g + 1))
        def mm():
            return [proj(part * CONV_A_WIDTH + MXU_COLS * g) for part in range(4)]
        def ew(r):
            h, c, b, z = r
            ya_ref[0, :, sl] = (b * conv_taps(ua_scr, ca_ref, c * h, sl) * _silu(z)).astype(BF16)
        return mm, ew

    def mixer_b_in(g):
        sl = slice(MXU_COLS * g, MXU_COLS * (g + 1))
        def mm():
            return proj(qkv_lo + MXU_COLS * g)
        def ew(r):
            a = _silu(conv_taps(qkv_scr, cb_ref, r, sl))
            for i in range(MXU_COLS // LANES):
                part, hj = divmod(g * (MXU_COLS // LANES) + i, GDN_HEADS)
                hs = slice(LANES * hj, LANES * (hj + 1))
                ah = a[:, LANES * i:LANES * (i + 1)]
                if part < 2:
                    ah = ah * lax.rsqrt(jnp.sum(ah * ah, axis=-1, keepdims=True) + L2_EPS)
                (q_ref, k_ref, v_ref)[part][0, :, hs] = ah.astype(BF16)
        return mm, ew

    def gate_b(g):
        sl = slice(MXU_COLS * g, MXU_COLS * (g + 1))
        def mm():
            return proj(zb_lo + MXU_COLS * g)
        def ew(r):
            zg_ref[0, :, sl] = _silu(r).astype(BF16)
        return mm, ew

    n_a, n_b = CONV_A_WIDTH // MXU_COLS, 3 * GDN_WIDTH // MXU_COLS
    for i in range(n_a):
        tasks.append(mixer_a(i))
        tasks += [mixer_b_in(g) for g in range(i * n_b // n_a, (i + 1) * n_b // n_a)]
        tasks.append(gate_b(i))
    for mm, ew in tasks:
        ew(mm())
    ba_ref[0] = jnp.dot(xt, wba_ref[...], preferred_element_type=F32)


def _even_in_call(x, w, wba, ca, cb, *, tm):
    bsz, s, _ = x.shape
    row = lambda w: pl.BlockSpec((1, tm, w), lambda b, t: (b, t, 0))
    return pl.pallas_call(
        functools.partial(_even_in_body, tm=tm),
        grid=(bsz, s // tm),
        in_specs=[row(D_MODEL), _const_spec(w.shape), _const_spec(wba.shape), _const_spec(ca.shape),
                  _const_spec(cb.shape)],
        out_specs=[row(CONV_A_WIDTH), row(GDN_WIDTH), row(GDN_WIDTH), row(GDN_WIDTH), row(GDN_WIDTH),
                   row(LANES)],
        out_shape=[jax.ShapeDtypeStruct((bsz, s, CONV_A_WIDTH), BF16),
                   jax.ShapeDtypeStruct((bsz, s, GDN_WIDTH), BF16),
                   jax.ShapeDtypeStruct((bsz, s, GDN_WIDTH), BF16),
                   jax.ShapeDtypeStruct((bsz, s, GDN_WIDTH), BF16),
                   jax.ShapeDtypeStruct((bsz, s, GDN_WIDTH), BF16),
                   jax.ShapeDtypeStruct((bsz, s, LANES), F32)],
        scratch_shapes=[pltpu.VMEM((tm + HALO, CONV_A_WIDTH), F32),
                        pltpu.VMEM((tm + HALO, 3 * GDN_WIDTH), F32)],
        compiler_params=pltpu.CompilerParams(dimension_semantics=("arbitrary", "arbitrary"),
                                             vmem_limit_bytes=VMEM_LIMIT),
        name="even_in_proj",
    )(x, w, wba, ca, cb)


def _gdn_body(q_ref, k_ref, v_ref, zg_ref, ba_ref, arow_ref, acol_ref, gn_ref,
              o_ref, s_scr, *, tb):
    nchunk = tb // CHUNK

    @pl.when(pl.program_id(1) == 0)
    def _():
        s_scr[...] = jnp.zeros(s_scr.shape, F32)

    ba = ba_ref[0]
    beta_cols = _sigmoid(ba)
    g_cols = -jnp.exp(arow_ref[0:1, :]) * _softplus(ba + arow_ref[1:2, :])
    ba_rows = jnp.transpose(ba)[0:2 * GDN_HEADS, :]
    g_rows = -jnp.exp(acol_ref[:, 0:1]) * _softplus(ba_rows + acol_ref[:, 1:2])
    ti = lax.broadcasted_iota(jnp.int32, (tb, tb), 0)
    tj = lax.broadcasted_iota(jnp.int32, (tb, tb), 1)
    same_chunk = (ti // CHUNK) == (tj // CHUNK)
    lower_blk = jnp.where(same_chunk & (tj <= ti), 1.0, 0.0).astype(F32)
    upper_blk = jnp.where(same_chunk & (ti <= tj), 1.0, 0.0).astype(F32)
    gc_cols = _mm_f32(lower_blk, g_cols)
    gc_rows = _mm_f32(g_rows, upper_blk)

    pair_w = 2 * LANES
    row = lax.broadcasted_iota(jnp.int32, (CHUNK, 2 * CHUNK), 0)
    lane = lax.broadcasted_iota(jnp.int32, (CHUNK, 2 * CHUNK), 1)
    col = lane & (CHUNK - 1)
    first = lane < CHUNK
    eye = jnp.where(row == col, 1.0, 0.0).astype(F32)
    level_masks = [_pair_mask(row, col, lb) for lb in range(int(math.log2(CHUNK)))]
    lane1 = lax.broadcasted_iota(jnp.int32, (1, LANES), 1)
    scale = HEAD_DIM ** -0.5
    gn = gn_ref[...]

    def blockdiag_c(a, keep=None):
        top = first if keep is None else keep & first
        bot = ~first if keep is None else keep & ~first
        return jnp.concatenate([jnp.where(top, a, 0.0), jnp.where(bot, a, 0.0)], axis=0).astype(BF16)

    def per_head(c0, c1):
        return jnp.concatenate([jnp.broadcast_to(c0, (CHUNK, LANES)), jnp.broadcast_to(c1, (CHUNK, LANES))],
                               axis=1)

    probs = [(hp, ci) for hp in range(GDN_HEADS // 2) for ci in range(nchunk)]
    pr = []
    for hp, ci in probs:
        h0, h1 = 2 * hp, 2 * hp + 1
        rs = slice(CHUNK * ci, CHUNK * (ci + 1))
        ws = slice(pair_w * hp, pair_w * (hp + 1))
        d = dict(hp=hp, rs=rs, ws=ws)
        q = q_ref[0, rs, ws].astype(F32) * scale
        d["k_b"] = k_ref[0, rs, ws]
        k = d["k_b"].astype(F32)
        v = v_ref[0, rs, ws].astype(F32)
        gcc = [gc_cols[rs, GDN_HEADS + h:GDN_HEADS + h + 1] for h in (h0, h1)]
        g_last = [g[CHUNK - 1:CHUNK, :] for g in gcc]
        tile = slice(LANES * (ci // 2), LANES * (ci // 2 + 1))
        r0 = gc_rows[GDN_HEADS + h0:GDN_HEADS + h0 + 1, tile]
        r1 = gc_rows[GDN_HEADS + h1:GDN_HEADS + h1 + 1, tile]
        if ci % 2:
            r0 = pltpu.roll(r0, CHUNK, axis=1)
        else:
            r1 = pltpu.roll(r1, CHUNK, axis=1)
        gcr = jnp.where(lane1 < CHUNK, r0, r1)
        gcc2 = jnp.where(first, gcc[0], gcc[1])
        d["decay"] = jnp.exp(jnp.where(row >= col, gcc2 - gcr, NEG_BIG))
        beta = per_head(beta_cols[rs, h0:h0 + 1], beta_cols[rs, h1:h1 + 1])
        e_gc = per_head(jnp.exp(gcc[0]), jnp.exp(gcc[1]))
        kb = k * beta
        d["kbq"] = jnp.concatenate([kb, q], axis=0).astype(BF16)
        vb, kbe = (v * beta).astype(BF16), (kb * e_gc).astype(BF16)
        z = jnp.zeros((CHUNK, pair_w), BF16)
        d["rhs"] = jnp.concatenate(
            [jnp.concatenate([vb[:, :LANES], kbe[:, :LANES], z], axis=1),
             jnp.concatenate([z, vb[:, LANES:], kbe[:, LANES:]], axis=1)], axis=0)
        d["q_dec"] = (q * e_gc).astype(BF16)
        d["k_dec"] = (k * per_head(jnp.exp(g_last[0] - gcc[0]), jnp.exp(g_last[1] - gcc[1]))).astype(BF16)
        d["s_decay"] = [jnp.exp(g) for g in g_last]
        pr.append(d)

    for d in pr:
        both = _mm_nt(d["kbq"], _blockdiag2(d["k_b"]))
        d["low"] = jnp.where(row > col, both[:CHUNK] * d["decay"], 0.0)
        d["attn"] = (both[CHUNK:] * d["decay"]).astype(BF16)
        d["tinv"] = eye - jnp.where(level_masks[0], d["low"], 0.0)
    for lb in range(1, len(level_masks)):
        for d in pr:
            d["p"] = _mm(d["tinv"], blockdiag_c(d["low"], level_masks[lb]))
        for d in pr:
            d["tinv"] = d["tinv"] - _mm(d["p"], blockdiag_c(d["tinv"]))
    for d in pr:
        d["uw"] = _mm(d["tinv"], d["rhs"])

    states = [s_scr[h] for h in range(GDN_HEADS)]
    for ci in range(nchunk):
        cur = [d for d, (_, c) in zip(pr, probs) if c == ci]
        for d in cur:
            d["wq_s"] = []
            for i in range(2):
                ls = slice(LANES * i, LANES * (i + 1))
                w = d["uw"][:, pair_w * i + LANES:pair_w * (i + 1)].astype(BF16)
                d["wq_s"].append(_mm(jnp.concatenate([w, d["q_dec"][:, ls]], axis=0), states[2 * d["hp"] + i]))
        for d in cur:
            v_new = jnp.concatenate([d["uw"][:, pair_w * i:pair_w * i + LANES] - d["wq_s"][i][:CHUNK]
                                     for i in range(2)], axis=1).astype(BF16)
            o_pair = (jnp.concatenate([d["wq_s"][i][CHUNK:] for i in range(2)], axis=1)
                      + _mm(d["attn"], _blockdiag2(v_new)))
            for i in range(2):
                h = 2 * d["hp"] + i
                ls = slice(LANES * i, LANES * (i + 1))
                hs = slice(LANES * h, LANES * (h + 1))
                states[h] = states[h] * d["s_decay"][i] + _mm_tn(d["k_dec"][:, ls], v_new[:, ls])
                o = o_pair[:, ls]
                o = o * lax.rsqrt(jnp.mean(o * o, axis=-1, keepdims=True) + NORM_EPS) * gn
                o_ref[0, d["rs"], hs] = (o * zg_ref[0, d["rs"], hs].astype(F32)).astype(BF16)
    for h in range(GDN_HEADS):
        s_scr[h] = states[h]


def _gdn_call(q, k, v, zg, ba, arow, acol, gn, *, tb):
    bsz, s, _ = q.shape
    row = lambda w: pl.BlockSpec((1, tb, w), lambda b, t: (b, t, 0))
    return pl.pallas_call(
        functools.partial(_gdn_body, tb=tb),
        grid=(bsz, s // tb),
        in_specs=[row(GDN_WIDTH), row(GDN_WIDTH), row(GDN_WIDTH), row(GDN_WIDTH), row(LANES),
                  _const_spec(arow.shape), _const_spec(acol.shape), _const_spec(gn.shape)],
        out_specs=row(GDN_WIDTH),
        out_shape=jax.ShapeDtypeStruct((bsz, s, GDN_WIDTH), BF16),
        scratch_shapes=[pltpu.VMEM((GDN_HEADS, HEAD_DIM, HEAD_DIM), F32)],
        compiler_params=pltpu.CompilerParams(dimension_semantics=("arbitrary", "arbitrary"),
                                             vmem_limit_bytes=VMEM_LIMIT),
        name="gated_delta_rule",
    )(q, k, v, zg, ba, arow, acol, gn)


def _out_body(*refs, n_y):
    y_refs, (x_ref, p_ref) = refs[:n_y], refs[n_y:n_y + 2]
    wo_refs = refs[n_y + 2:2 * n_y + 2]
    wg_ref, wp_ref, lg_ref, lb_ref, o_ref = refs[2 * n_y + 2:]
    tm = x_ref.shape[0]
    subs = [slice(r, r + OUT_SUB_ROWS) for r in range(0, tm, OUT_SUB_ROWS)]
    s_parts, embs, xns, gates = [], [], [], []
    for rs in subs:
        s = jnp.dot(y_refs[0][rs, :], wo_refs[0][...], preferred_element_type=F32)
        for y_ref, wo_ref in zip(y_refs[1:], wo_refs[1:]):
            s = s + jnp.dot(y_ref[rs, :], wo_ref[...], preferred_element_type=F32)
        s_parts.append(s)
    for rs in subs:
        embs.append(jnp.dot(p_ref[rs, :].astype(BF16), wp_ref[...], preferred_element_type=F32))
    for rs, s in zip(subs, s_parts):
        t = DEEPNORM_ALPHA * x_ref[rs, :] + s
        mu = jnp.mean(t, axis=-1, keepdims=True)
        tc = t - mu
        var = jnp.mean(tc * tc, axis=-1, keepdims=True)
        xn = tc * lax.rsqrt(var + NORM_EPS) * lg_ref[...] + lb_ref[...]
        xns.append(xn)
        gates.append(jnp.dot(xn.astype(BF16), wg_ref[...], preferred_element_type=F32))
    for rs, xn, emb, g in zip(subs, xns, embs, gates):
        o_ref[rs, :] = xn + emb * _sigmoid(g)


def _out_call(ys, x, p, layer, wos, wg, wp, lg, lb, *, tm):
    n, d = x.shape
    row = lambda w: pl.BlockSpec((tm, w), lambda i: (i, 0))
    p_spec = pl.BlockSpec((None, tm, p.shape[2]), lambda i: (layer, i, 0))
    return pl.pallas_call(
        functools.partial(_out_body, n_y=len(ys)),
        grid=(n // tm,),
        in_specs=([row(y.shape[1]) for y in ys] + [row(d), p_spec]
                  + [_const_spec(w.shape) for w in wos]
                  + [_const_spec(wg.shape), _const_spec(wp.shape), _const_spec(lg.shape), _const_spec(lb.shape)]),
        out_specs=row(d),
        out_shape=jax.ShapeDtypeStruct((n, d), F32),
        compiler_params=pltpu.CompilerParams(dimension_semantics=("arbitrary",),
                                             vmem_limit_bytes=VMEM_LIMIT),
        name="out_proj_norm_gate",
    )(*ys, x, p, *wos, wg, wp, lg, lb)


def _odd_in_body(x_ref, w_ref, lbr_ref, q_ref, f_ref, v_ref, zg_ref):
    xt = x_ref[...].astype(BF16)
    lb_raw = lbr_ref[...]
    e = jnp.exp(lb_raw - jnp.max(lb_raw, axis=0, keepdims=True))
    sm = e / jnp.sum(e, axis=0, keepdims=True)
    lower = (sm[0:1, :] + sm[1:2, :]) - sm[0:1, :]
    w = HGRN_WIDTH
    q_ref[...] = _silu(jnp.dot(xt, w_ref[:, 0:w], preferred_element_type=F32)).astype(BF16)
    f_raw = jnp.dot(xt, w_ref[:, w:2 * w], preferred_element_type=F32)
    f_ref[...] = lower + (1.0 - lower) * _sigmoid(f_raw)
    v_ref[...] = jnp.dot(xt, w_ref[:, 2 * w:3 * w], preferred_element_type=F32).astype(BF16)
    zg_ref[...] = _silu(jnp.dot(xt, w_ref[:, 3 * w:4 * w], preferred_element_type=F32)).astype(BF16)


def _odd_in_call(x, w, lbr, *, tm):
    n, d = x.shape
    row = lambda wd: pl.BlockSpec((tm, wd), lambda i: (i, 0))
    return pl.pallas_call(
        _odd_in_body,
        grid=(n // tm,),
        in_specs=[row(d), _const_spec(w.shape), _const_spec(lbr.shape)],
        out_specs=[row(HGRN_WIDTH)] * 4,
        out_shape=[jax.ShapeDtypeStruct((n, HGRN_WIDTH), BF16),
                   jax.ShapeDtypeStruct((n, HGRN_WIDTH), F32),
                   jax.ShapeDtypeStruct((n, HGRN_WIDTH), BF16),
                   jax.ShapeDtypeStruct((n, HGRN_WIDTH), BF16)],
        compiler_params=pltpu.CompilerParams(dimension_semantics=("arbitrary",),
                                             vmem_limit_bytes=VMEM_LIMIT),
        name="odd_in_proj",
    )(x, w, lbr)


GROUP = SUBLANES
NGROUP = CHUNK // GROUP


def _group_row_masks(w):
    r = lax.broadcasted_iota(jnp.int32, (NGROUP, GROUP, w), 1)
    return dict(odd=(r & 1) == 1, hi2=(r & 2) != 0, hi4=(r & 4) != 0)


def _decay_products(f, m):
    w = f.shape[-1]
    f3 = f.reshape(NGROUP, GROUP, w)
    odd, hi2, hi4 = m["odd"], m["hi2"], m["hi4"]
    on_rows = lambda x, i: jnp.broadcast_to(x[:, i:i + 1, :], x.shape)
    p2 = f3 * jnp.where(odd, pltpu.roll(f3, 1, axis=1), 1.0)
    p4 = p2 * jnp.where(hi2, jnp.where(hi4, on_rows(p2, 5), on_rows(p2, 1)), 1.0)
    p8 = p4 * jnp.where(hi4, on_rows(p4, 3), 1.0)
    s2 = jnp.where(odd, 1.0, pltpu.roll(f3, GROUP - 1, axis=1))
    s4 = s2 * jnp.where(hi2, 1.0, jnp.where(hi4, on_rows(p2, 7), on_rows(p2, 3)))
    s8 = s4 * jnp.where(hi4, 1.0, on_rows(p4, 7))
    fac = {1: f3, 2: jnp.where(hi2, p2, s2), 4: jnp.where(hi4, p4, s4)}
    fac = {b: e.reshape(CHUNK, w) for b, e in fac.items()}

    tot = [p8[g, GROUP - 1:GROUP, :] for g in range(NGROUP)]

    def prefix_scaled(lo, hi):
        out, run = [p8[lo]], None
        for g in range(lo + 1, hi):
            run = tot[g - 1] if run is None else run * tot[g - 1]
            out.append(p8[g] * run)
        return out, (tot[hi - 1] if run is None else run * tot[hi - 1])

    def suffix_scaled(lo, hi):
        out, run = [s8[hi - 1]], None
        for g in range(hi - 2, lo - 1, -1):
            run = tot[g + 1] if run is None else run * tot[g + 1]
            out.append(s8[g] * run)
        return out[::-1]

    b = GROUP
    while b < CHUNK:
        gpb = b // GROUP
        pieces = []
        for blk in range(NGROUP // gpb):
            lo, hi = blk * gpb, (blk + 1) * gpb
            pieces += prefix_scaled(lo, hi)[0] if blk % 2 else suffix_scaled(lo, hi)
        fac[b] = jnp.concatenate(pieces, axis=0)
        b *= 2
    inc, total = prefix_scaled(0, NGROUP)
    exc = suffix_scaled(0, NGROUP)
    return fac, jnp.concatenate(inc, axis=0), jnp.concatenate(exc, axis=0), total


def _pair_operand(q, k, e, log2_b, upper_rows):
    b = 1 << log2_b
    if b >= GROUP:
        gpb = b // GROUP
        qk = jnp.concatenate([((q if (g // gpb) % 2 else k)[GROUP * g:GROUP * (g + 1)])
                              for g in range(NGROUP)], axis=0)
    elif b == 1:
        return jnp.where(upper_rows, q * e, k).astype(BF16)
    else:
        qk = jnp.where(upper_rows, q, k)
    return (qk * e).astype(BF16)


def _blockdiag2(a):
    z = jnp.zeros((a.shape[0], LANES), a.dtype)
    return jnp.concatenate([jnp.concatenate([a[:, :LANES], z], axis=1),
                            jnp.concatenate([z, a[:, LANES:]], axis=1)], axis=0)


def _hgrn_body(q_ref, f_ref, v_ref, zg_ref, gn_ref, o_ref, s_scr, *, tb, hb):
    nchunk = tb // CHUNK
    nlev = int(math.log2(CHUNK))
    pair_w = 2 * LANES

    @pl.when(pl.program_id(2) == 0)
    def _():
        s_scr[...] = jnp.zeros(s_scr.shape, F32)

    row = lax.broadcasted_iota(jnp.int32, (CHUNK, 2 * CHUNK), 0)
    col = lax.broadcasted_iota(jnp.int32, (CHUNK, 2 * CHUNK), 1) & (CHUNK - 1)
    level_masks = [_pair_mask(row, col, lb) for lb in range(nlev)]
    trow = lax.broadcasted_iota(jnp.int32, (CHUNK, pair_w), 0)
    upper_rows = [(jnp.right_shift(trow, lb) & 1) == 1 for lb in range(int(math.log2(GROUP)))]
    group_masks = _group_row_masks(pair_w)
    gn = gn_ref[...]

    probs = [(hp, ci) for hp in range(hb // 2) for ci in range(nchunk)]
    pr = []
    for hp, ci in probs:
        d = dict(ws=slice(pair_w * hp, pair_w * (hp + 1)), rs=slice(CHUNK * ci, CHUNK * (ci + 1)))
        f = f_ref[0, d["rs"], d["ws"]]
        q = q_ref[0, d["rs"], d["ws"]].astype(F32)
        k = 1.0 - f
        d["v_b"] = v_ref[0, d["rs"], d["ws"]]
        fac, inc, exc, d["s_decay"] = _decay_products(f, group_masks)
        d["q_b"], d["k_b"] = q.astype(BF16), k.astype(BF16)
        d["ops"] = [_pair_operand(q, k, fac[1 << lb], lb, upper_rows[lb] if lb < len(upper_rows) else None)
                    for lb in range(nlev)]
        d["q_inc"] = (q * inc).astype(BF16)
        d["k_exc"] = (k * exc).astype(BF16)
        pr.append(d)
    for d in pr:
        d["attn"] = jnp.where(row == col, _mm_nt(d["q_b"], _blockdiag2(d["k_b"])), 0.0)
    for lb in range(nlev):
        for d in pr:
            p = d["ops"][lb]
            d["attn"] = jnp.where(level_masks[lb], _mm_nt(p, _blockdiag2(p)), d["attn"])
    for d in pr:
        d["o_intra"] = _mm(d["attn"], _blockdiag2(d["v_b"]))
        d["kv"] = [_mm_tn(d["v_b"][:, LANES * i:LANES * (i + 1)], d["k_exc"][:, LANES * i:LANES * (i + 1)])
                   for i in range(2)]

    for hp in range(hb // 2):
        states = [s_scr[2 * hp + i] for i in range(2)]
        for d, (php, _) in zip(pr, probs):
            if php != hp:
                continue
            for i in range(2):
                ls = slice(LANES * i, LANES * (i + 1))
                o = d["o_intra"][:, ls] + _mm_nt(d["q_inc"][:, ls], states[i])
                states[i] = states[i] * d["s_decay"][:, ls] + d["kv"][i]
                o = o * lax.rsqrt(jnp.mean(o * o, axis=-1, keepdims=True) + NORM_EPS) * gn
                os = slice(pair_w * hp + LANES * i, pair_w * hp + LANES * (i + 1))
                o_ref[0, d["rs"], os] = (o * zg_ref[0, d["rs"], os].astype(F32)).astype(BF16)
        for i in range(2):
            s_scr[2 * hp + i] = states[i]


def _hgrn_call(q, f, v, zg, gn, *, tb, hb):
    bsz, s, w = q.shape
    blk = lambda: pl.BlockSpec((1, tb, hb * LANES), lambda b, g, t: (b, t, g))
    return pl.pallas_call(
        functools.partial(_hgrn_body, tb=tb, hb=hb),
        grid=(bsz, w // (hb * LANES), s // tb),
        in_specs=[blk(), blk(), blk(), blk(), _const_spec(gn.shape)],
        out_specs=blk(),
        out_shape=jax.ShapeDtypeStruct((bsz, s, w), BF16),
        scratch_shapes=[pltpu.VMEM((hb, HEAD_DIM, HEAD_DIM), F32)],
        compiler_params=pltpu.CompilerParams(dimension_semantics=("arbitrary", "arbitrary", "arbitrary"),
                                             vmem_limit_bytes=VMEM_LIMIT),
        name="hgrn2_recurrence",
    )(q, f, v, zg, gn)


def kernel(x, p, w_in_even, conv_a_w, conv_b_w, a_log, dt_bias, gdn_norm_g, w_out_even, w_in_odd,
           lower_bounds, hgrn_norm_g, w_out_odd, ln_g, ln_b, w_pl, w_pl_gate):
    bsz, s, d = x.shape
    n = bsz * s
    zb_end = 4 * CONV_A_WIDTH + 4 * GDN_WIDTH

    w_in = w_in_even[0].astype(BF16)
    wba = jnp.pad(w_in[:, zb_end:], ((0, 0), (0, LANES - 2 * GDN_HEADS)))
    ya, q, k, v, zg, ba = _even_in_call(x, w_in[:, :zb_end], wba, conv_a_w[0], conv_b_w[0], tm=256)

    pad_lane = lambda a: jnp.pad(a, (GDN_HEADS, LANES - 2 * GDN_HEADS))
    arow = jnp.stack([pad_lane(a_log[0]), pad_lane(dt_bias[0])], axis=0)
    acol = jnp.stack([jnp.pad(a_log[0], (GDN_HEADS, 0)), jnp.pad(dt_bias[0], (GDN_HEADS, 0))], axis=1)
    og = _gdn_call(q, k, v, zg, ba, arow, acol, gdn_norm_g[0][None, :], tb=256)

    w_out = w_out_even[0].astype(BF16)
    p_rows = p.reshape(DEPTH, n, PL_DIM)
    x1 = _out_call([ya.reshape(n, -1), og.reshape(n, -1)], x.reshape(n, d), p_rows, 0,
                   [w_out[:CONV_A_WIDTH], w_out[CONV_A_WIDTH:]], w_pl_gate[0].astype(BF16),
                   w_pl[0].astype(BF16), ln_g[0][None, :], ln_b[0][None, :], tm=512)

    hq, hf, hv, hzg = _odd_in_call(x1, w_in_odd[0].astype(BF16), lower_bounds, tm=256)
    shp = (bsz, s, HGRN_WIDTH)
    ho = _hgrn_call(hq.reshape(shp), hf.reshape(shp), hv.reshape(shp), hzg.reshape(shp),
                    hgrn_norm_g[0][None, :], tb=512, hb=8)
    out = _out_call([ho.reshape(n, -1)], x1, p_rows, 1, [w_out_odd[0].astype(BF16)],
                    w_pl_gate[1].astype(BF16), w_pl[1].astype(BF16), ln_g[1][None, :], ln_b[1][None, :],
                    tm=512)
    return out.reshape(bsz, s, d)
```

```python
import functools
import math

import jax
import jax.numpy as jnp
from jax import lax
from jax.experimental import pallas as pl
from jax.experimental.pallas import tpu as pltpu

F32 = jnp.float32
BF16 = jnp.bfloat16

D_MODEL = 1024
DEPTH = 2
PL_DIM = 256
CONV_A_WIDTH = 1024
CONV_A_KERNEL = 3
GDN_HEADS = 8
HEAD_DIM = 128
GDN_WIDTH = GDN_HEADS * HEAD_DIM
GDN_CONV_KERNEL = 4
HGRN_WIDTH = 2 * D_MODEL
HGRN_HEADS = HGRN_WIDTH // HEAD_DIM
DEEPNORM_ALPHA = (2.0 * DEPTH) ** 0.25
NORM_EPS = 1e-5
L2_EPS = 1e-6

LANES = 128
SUBLANES = 8
MXU_COLS = 256
OUT_SUB_ROWS = 128
CHUNK = 64
HALO = SUBLANES
NEG_BIG = -1e30

VMEM_CAPACITY = 64 * 1024 * 1024
VMEM_LIMIT = VMEM_CAPACITY * 7 // 8

IN_ROWS = 256
OUT_ROWS = 512
GDN_ROWS, GDN_BATCH_ROWS = 256, 2
HGRN_ROWS, HGRN_STEP_HEADS = 512, 8


def _sigmoid(x):
    return 1.0 / (1.0 + jnp.exp(-x))


def _silu(x):
    return x * _sigmoid(x)


def _softplus(x):
    return jnp.maximum(x, 0.0) + jnp.log1p(jnp.exp(-jnp.abs(x)))


def _mm(a, b):
    return jnp.dot(a.astype(BF16), b.astype(BF16), preferred_element_type=F32)


def _mm_nt(a, b):
    return lax.dot_general(a.astype(BF16), b.astype(BF16), (((1,), (1,)), ((), ())),
                           preferred_element_type=F32)


def _mm_tn(a, b):
    return lax.dot_general(a.astype(BF16), b.astype(BF16), (((0,), (0,)), ((), ())),
                           preferred_element_type=F32)


def _mm_f32(a, b):
    return jnp.dot(a, b, preferred_element_type=F32, precision=lax.Precision.HIGHEST)


def _const_spec(shape):
    nd = len(shape)
    return pl.BlockSpec(shape, lambda *_: (0,) * nd, pipeline_mode=pl.Buffered(1))


def _pair_mask(row, col, log2_b):
    return (jnp.right_shift(jnp.bitwise_xor(row, col), log2_b) == 1) & (row > col)


def _blockdiag2(a):
    z = jnp.zeros((a.shape[0], LANES), a.dtype)
    return jnp.concatenate([jnp.concatenate([a[:, :LANES], z], axis=1),
                            jnp.concatenate([z, a[:, LANES:]], axis=1)], axis=0)


def _even_in_body(x_ref, w_ref, wba_ref, ca_ref, cb_ref,
                  ya_ref, q_ref, k_ref, v_ref, zg_ref, ba_ref, ua_scr, qkv_scr, *, tm):
    @pl.when(pl.program_id(1) == 0)
    def _():
        ua_scr[0:HALO, :] = jnp.zeros((HALO, CONV_A_WIDTH), F32)
        qkv_scr[0:HALO, :] = jnp.zeros((HALO, 3 * GDN_WIDTH), F32)

    xt = x_ref[0].astype(BF16)
    proj = lambda lo: jnp.dot(xt, w_ref[:, lo:lo + MXU_COLS], preferred_element_type=F32)
    qkv_lo, zb_lo = 4 * CONV_A_WIDTH, 4 * CONV_A_WIDTH + 3 * GDN_WIDTH

    def conv_taps(scr, w_ref, cur, sl):
        taps = w_ref.shape[0]
        scr[HALO:HALO + tm, sl] = cur
        ext = scr[0:HALO + tm, sl]
        acc = w_ref[taps - 1:taps, sl] * cur
        for j in range(1, taps):
            acc = acc + w_ref[taps - 1 - j:taps - j, sl] * pltpu.roll(ext, j, axis=0)[HALO:]
        scr[0:HALO, sl] = scr[tm:tm + HALO, sl]
        return acc

    tasks = []

    def mixer_a(g):
        sl = slice(MXU_COLS * g, MXU_COLS * (g + 1))
        def mm():
            return [proj(part * CONV_A_WIDTH + MXU_COLS * g) for part in range(4)]
        def ew(r):
            h, c, b, z = r
            ya_ref[0, :, sl] = (b * conv_taps(ua_scr, ca_ref, c * h, sl) * _silu(z)).astype(BF16)
        return mm, ew

    def mixer_b_in(g):
        sl = slice(MXU_COLS * g, MXU_COLS * (g + 1))
        def mm():
            return proj(qkv_lo + MXU_COLS * g)
        def ew(r):
            a = _silu(conv_taps(qkv_scr, cb_ref, r, sl))
            for i in range(MXU_COLS // LANES):
                part, hj = divmod(g * (MXU_COLS // LANES) + i, GDN_HEADS)
                hs = slice(LANES * hj, LANES * (hj + 1))
                ah = a[:, LANES * i:LANES * (i + 1)]
                if part < 2:
                    ah = ah * lax.rsqrt(jnp.sum(ah * ah, axis=-1, keepdims=True) + L2_EPS)
                (q_ref, k_ref, v_ref)[part][0, :, hs] = ah.astype(BF16)
        return mm, ew

    def gate_b(g):
        sl = slice(MXU_COLS * g, MXU_COLS * (g + 1))
        def mm():
            return proj(zb_lo + MXU_COLS * g)
        def ew(r):
            zg_ref[0, :, sl] = _silu(r).astype(BF16)
        return mm, ew

    n_a, n_b = CONV_A_WIDTH // MXU_COLS, 3 * GDN_WIDTH // MXU_COLS
    for i in range(n_a):
        tasks.append(mixer_a(i))
        tasks += [mixer_b_in(g) for g in range(i * n_b // n_a, (i + 1) * n_b // n_a)]
        tasks.append(gate_b(i))
    for mm, ew in tasks:
        ew(mm())
    ba_ref[0] = jnp.dot(xt, wba_ref[...], preferred_element_type=F32)


def _even_in_call(x, w, wba, ca, cb, *, tm):
    bsz, s, _ = x.shape
    row = lambda w: pl.BlockSpec((1, tm, w), lambda b, t: (b, t, 0))
    return pl.pallas_call(
        functools.partial(_even_in_body, tm=tm),
        grid=(bsz, s // tm),
        in_specs=[row(D_MODEL), _const_spec(w.shape), _const_spec(wba.shape), _const_spec(ca.shape),
                  _const_spec(cb.shape)],
        out_specs=[row(CONV_A_WIDTH), row(GDN_WIDTH), row(GDN_WIDTH), row(GDN_WIDTH), row(GDN_WIDTH),
                   row(LANES)],
        out_shape=[jax.ShapeDtypeStruct((bsz, s, CONV_A_WIDTH), BF16),
                   jax.ShapeDtypeStruct((bsz, s, GDN_WIDTH), BF16),
                   jax.ShapeDtypeStruct((bsz, s, GDN_WIDTH), BF16),
                   jax.ShapeDtypeStruct((bsz, s, GDN_WIDTH), BF16),
                   jax.ShapeDtypeStruct((bsz, s, GDN_WIDTH), BF16),
                   jax.ShapeDtypeStruct((bsz, s, LANES), F32)],
        scratch_shapes=[pltpu.VMEM((tm + HALO, CONV_A_WIDTH), F32),
                        pltpu.VMEM((tm + HALO, 3 * GDN_WIDTH), F32)],
        compiler_params=pltpu.CompilerParams(dimension_semantics=("arbitrary", "arbitrary"),
                                             vmem_limit_bytes=VMEM_LIMIT),
        name="even_in_proj",
    )(x, w, wba, ca, cb)


def _gdn_body(q_ref, k_ref, v_ref, zg_ref, ba_ref, arow_ref, acol_ref, gn_ref,
              o_ref, s_scr, *, tb, bb):
    nchunk = tb // CHUNK

    @pl.when(pl.program_id(1) == 0)
    def _():
        s_scr[...] = jnp.zeros(s_scr.shape, F32)

    ti = lax.broadcasted_iota(jnp.int32, (tb, tb), 0)
    tj = lax.broadcasted_iota(jnp.int32, (tb, tb), 1)
    same_chunk = (ti // CHUNK) == (tj // CHUNK)
    lower_blk = jnp.where(same_chunk & (tj <= ti), 1.0, 0.0).astype(F32)
    upper_blk = jnp.where(same_chunk & (ti <= tj), 1.0, 0.0).astype(F32)
    beta_cols, gc_cols, gc_rows = [], [], []
    for bi in range(bb):
        ba = ba_ref[bi]
        beta_cols.append(_sigmoid(ba))
        g_cols = -jnp.exp(arow_ref[0:1, :]) * _softplus(ba + arow_ref[1:2, :])
        ba_rows = jnp.transpose(ba)[0:2 * GDN_HEADS, :]
        g_rows = -jnp.exp(acol_ref[:, 0:1]) * _softplus(ba_rows + acol_ref[:, 1:2])
        gc_cols.append(_mm_f32(lower_blk, g_cols))
        gc_rows.append(_mm_f32(g_rows, upper_blk))

    pair_w = 2 * LANES
    row = lax.broadcasted_iota(jnp.int32, (CHUNK, 2 * CHUNK), 0)
    lane = lax.broadcasted_iota(jnp.int32, (CHUNK, 2 * CHUNK), 1)
    col = lane & (CHUNK - 1)
    first = lane < CHUNK
    eye = jnp.where(row == col, 1.0, 0.0).astype(F32)
    level_masks = [_pair_mask(row, col, lb) for lb in range(int(math.log2(CHUNK)))]
    lane1 = lax.broadcasted_iota(jnp.int32, (1, LANES), 1)
    scale = HEAD_DIM ** -0.5
    gn = gn_ref[...]

    def blockdiag_c(a, keep=None):
        top = first if keep is None else keep & first
        bot = ~first if keep is None else keep & ~first
        return jnp.concatenate([jnp.where(top, a, 0.0), jnp.where(bot, a, 0.0)], axis=0).astype(BF16)

    def per_head(c0, c1):
        return jnp.concatenate([jnp.broadcast_to(c0, (CHUNK, LANES)), jnp.broadcast_to(c1, (CHUNK, LANES))],
                               axis=1)

    probs = [(bi, hp, ci) for bi in range(bb) for hp in range(GDN_HEADS // 2) for ci in range(nchunk)]
    pr = []
    for bi, hp, ci in probs:
        h0, h1 = 2 * hp, 2 * hp + 1
        rs = slice(CHUNK * ci, CHUNK * (ci + 1))
        ws = slice(pair_w * hp, pair_w * (hp + 1))
        d = dict(bi=bi, hp=hp, rs=rs)
        q = q_ref[bi, rs, ws].astype(F32) * scale
        d["k_b"] = k_ref[bi, rs, ws]
        k = d["k_b"].astype(F32)
        v = v_ref[bi, rs, ws].astype(F32)
        gcc = [gc_cols[bi][rs, GDN_HEADS + h:GDN_HEADS + h + 1] for h in (h0, h1)]
        g_last = [g[CHUNK - 1:CHUNK, :] for g in gcc]
        tile = slice(LANES * (ci // 2), LANES * (ci // 2 + 1))
        r0 = gc_rows[bi][GDN_HEADS + h0:GDN_HEADS + h0 + 1, tile]
        r1 = gc_rows[bi][GDN_HEADS + h1:GDN_HEADS + h1 + 1, tile]
        if ci % 2:
            r0 = pltpu.roll(r0, CHUNK, axis=1)
        else:
            r1 = pltpu.roll(r1, CHUNK, axis=1)
        gcr = jnp.where(lane1 < CHUNK, r0, r1)
        gcc2 = jnp.where(first, gcc[0], gcc[1])
        d["decay"] = jnp.exp(jnp.where(row >= col, gcc2 - gcr, NEG_BIG))
        beta = per_head(beta_cols[bi][rs, h0:h0 + 1], beta_cols[bi][rs, h1:h1 + 1])
        e_gc = per_head(jnp.exp(gcc[0]), jnp.exp(gcc[1]))
        kb = k * beta
        d["kbq"] = jnp.concatenate([kb, q], axis=0).astype(BF16)
        vb, kbe = (v * beta).astype(BF16), (kb * e_gc).astype(BF16)
        z = jnp.zeros((CHUNK, pair_w), BF16)
        d["rhs"] = jnp.concatenate(
            [jnp.concatenate([vb[:, :LANES], kbe[:, :LANES], z], axis=1),
             jnp.concatenate([z, vb[:, LANES:], kbe[:, LANES:]], axis=1)], axis=0)
        d["q_dec"] = (q * e_gc).astype(BF16)
        d["k_dec"] = (k * per_head(jnp.exp(g_last[0] - gcc[0]), jnp.exp(g_last[1] - gcc[1]))).astype(BF16)
        d["s_decay"] = [jnp.exp(g) for g in g_last]
        pr.append(d)

    for d in pr:
        both = _mm_nt(d["kbq"], _blockdiag2(d["k_b"]))
        d["low"] = jnp.where(row > col, both[:CHUNK] * d["decay"], 0.0)
        d["attn"] = (both[CHUNK:] * d["decay"]).astype(BF16)
        d["tinv"] = eye - jnp.where(level_masks[0], d["low"], 0.0)
    for lb in range(1, len(level_masks)):
        for d in pr:
            d["p"] = _mm(d["tinv"], blockdiag_c(d["low"], level_masks[lb]))
        for d in pr:
            d["tinv"] = d["tinv"] - _mm(d["p"], blockdiag_c(d["tinv"]))
    for d in pr:
        d["uw"] = _mm(d["tinv"], d["rhs"])

    states = [s_scr[i] for i in range(bb * GDN_HEADS)]
    for ci in range(nchunk):
        cur = [d for d, (_, _, c) in zip(pr, probs) if c == ci]
        for d in cur:
            d["wq_s"] = []
            for i in range(2):
                ls = slice(LANES * i, LANES * (i + 1))
                w = d["uw"][:, pair_w * i + LANES:pair_w * (i + 1)].astype(BF16)
                d["wq_s"].append(_mm(jnp.concatenate([w, d["q_dec"][:, ls]], axis=0),
                                     states[d["bi"] * GDN_HEADS + 2 * d["hp"] + i]))
        for d in cur:
            v_new = jnp.concatenate([d["uw"][:, pair_w * i:pair_w * i + LANES] - d["wq_s"][i][:CHUNK]
                                     for i in range(2)], axis=1).astype(BF16)
            o_pair = (jnp.concatenate([d["wq_s"][i][CHUNK:] for i in range(2)], axis=1)
                      + _mm(d["attn"], _blockdiag2(v_new)))
            for i in range(2):
                h = 2 * d["hp"] + i
                si = d["bi"] * GDN_HEADS + h
                ls = slice(LANES * i, LANES * (i + 1))
                hs = slice(LANES * h, LANES * (h + 1))
                states[si] = states[si] * d["s_decay"][i] + _mm_tn(d["k_dec"][:, ls], v_new[:, ls])
                o = o_pair[:, ls]
                o = o * lax.rsqrt(jnp.mean(o * o, axis=-1, keepdims=True) + NORM_EPS) * gn
                o_ref[d["bi"], d["rs"], hs] = (o * zg_ref[d["bi"], d["rs"], hs].astype(F32)).astype(BF16)
    for i in range(bb * GDN_HEADS):
        s_scr[i] = states[i]


def _gdn_call(q, k, v, zg, ba, arow, acol, gn, *, tb, bb):
    bsz, s, _ = q.shape
    row = lambda w: pl.BlockSpec((bb, tb, w), lambda b, t: (b, t, 0))
    return pl.pallas_call(
        functools.partial(_gdn_body, tb=tb, bb=bb),
        grid=(bsz // bb, s // tb),
        in_specs=[row(GDN_WIDTH), row(GDN_WIDTH), row(GDN_WIDTH), row(GDN_WIDTH), row(LANES),
                  _const_spec(arow.shape), _const_spec(acol.shape), _const_spec(gn.shape)],
        out_specs=row(GDN_WIDTH),
        out_shape=jax.ShapeDtypeStruct((bsz, s, GDN_WIDTH), BF16),
        scratch_shapes=[pltpu.VMEM((bb * GDN_HEADS, HEAD_DIM, HEAD_DIM), F32)],
        compiler_params=pltpu.CompilerParams(dimension_semantics=("arbitrary", "arbitrary"),
                                             vmem_limit_bytes=VMEM_LIMIT),
        name="gated_delta_rule",
    )(q, k, v, zg, ba, arow, acol, gn)


def _out_body(*refs, n_y):
    y_refs, (x_ref, p_ref) = refs[:n_y], refs[n_y:n_y + 2]
    wo_refs = refs[n_y + 2:2 * n_y + 2]
    wg_ref, wp_ref, lg_ref, lb_ref, o_ref = refs[2 * n_y + 2:]
    tm = x_ref.shape[0]
    subs = [slice(r, r + OUT_SUB_ROWS) for r in range(0, tm, OUT_SUB_ROWS)]
    s_parts, embs, xns, gates = [], [], [], []
    for rs in subs:
        s = jnp.dot(y_refs[0][rs, :], wo_refs[0][...], preferred_element_type=F32)
        for y_ref, wo_ref in zip(y_refs[1:], wo_refs[1:]):
            s = s + jnp.dot(y_ref[rs, :], wo_ref[...], preferred_element_type=F32)
        s_parts.append(s)
    for rs in subs:
        embs.append(jnp.dot(p_ref[rs, :].astype(BF16), wp_ref[...], preferred_element_type=F32))
    for rs, s in zip(subs, s_parts):
        t = DEEPNORM_ALPHA * x_ref[rs, :] + s
        mu = jnp.mean(t, axis=-1, keepdims=True)
        tc = t - mu
        var = jnp.mean(tc * tc, axis=-1, keepdims=True)
        xn = tc * lax.rsqrt(var + NORM_EPS) * lg_ref[...] + lb_ref[...]
        xns.append(xn)
        gates.append(jnp.dot(xn.astype(BF16), wg_ref[...], preferred_element_type=F32))
    for rs, xn, emb, g in zip(subs, xns, embs, gates):
        o_ref[rs, :] = xn + emb * _sigmoid(g)


def _out_call(ys, x, p, layer, wos, wg, wp, lg, lb, *, tm):
    n, d = x.shape
    row = lambda w: pl.BlockSpec((tm, w), lambda i: (i, 0))
    p_spec = pl.BlockSpec((None, tm, p.shape[2]), lambda i: (layer, i, 0))
    return pl.pallas_call(
        functools.partial(_out_body, n_y=len(ys)),
        grid=(n // tm,),
        in_specs=([row(y.shape[1]) for y in ys] + [row(d), p_spec]
                  + [_const_spec(w.shape) for w in wos]
                  + [_const_spec(wg.shape), _const_spec(wp.shape), _const_spec(lg.shape), _const_spec(lb.shape)]),
        out_specs=row(d),
        out_shape=jax.ShapeDtypeStruct((n, d), F32),
        compiler_params=pltpu.CompilerParams(dimension_semantics=("arbitrary",),
                                             vmem_limit_bytes=VMEM_LIMIT),
        name="out_proj_norm_gate",
    )(*ys, x, p, *wos, wg, wp, lg, lb)


def _odd_in_body(x_ref, w_ref, lbr_ref, q_ref, f_ref, v_ref, zg_ref):
    xt = x_ref[...].astype(BF16)
    lb_raw = lbr_ref[...]
    e = jnp.exp(lb_raw - jnp.max(lb_raw, axis=0, keepdims=True))
    sm = e / jnp.sum(e, axis=0, keepdims=True)
    lower = (sm[0:1, :] + sm[1:2, :]) - sm[0:1, :]
    w = HGRN_WIDTH
    q_ref[...] = _silu(jnp.dot(xt, w_ref[:, 0:w], preferred_element_type=F32)).astype(BF16)
    f_raw = jnp.dot(xt, w_ref[:, w:2 * w], preferred_element_type=F32)
    f_ref[...] = lower + (1.0 - lower) * _sigmoid(f_raw)
    v_ref[...] = jnp.dot(xt, w_ref[:, 2 * w:3 * w], preferred_element_type=F32).astype(BF16)
    zg_ref[...] = _silu(jnp.dot(xt, w_ref[:, 3 * w:4 * w], preferred_element_type=F32)).astype(BF16)


def _odd_in_call(x, w, lbr, *, tm):
    n, d = x.shape
    row = lambda wd: pl.BlockSpec((tm, wd), lambda i: (i, 0))
    return pl.pallas_call(
        _odd_in_body,
        grid=(n // tm,),
        in_specs=[row(d), _const_spec(w.shape), _const_spec(lbr.shape)],
        out_specs=[row(HGRN_WIDTH)] * 4,
        out_shape=[jax.ShapeDtypeStruct((n, HGRN_WIDTH), BF16),
                   jax.ShapeDtypeStruct((n, HGRN_WIDTH), F32),
                   jax.ShapeDtypeStruct((n, HGRN_WIDTH), BF16),
                   jax.ShapeDtypeStruct((n, HGRN_WIDTH), BF16)],
        compiler_params=pltpu.CompilerParams(dimension_semantics=("arbitrary",),
                                             vmem_limit_bytes=VMEM_LIMIT),
        name="odd_in_proj",
    )(x, w, lbr)


GROUP = SUBLANES
NGROUP = CHUNK // GROUP


def _group_row_masks(w):
    r = lax.broadcasted_iota(jnp.int32, (NGROUP, GROUP, w), 1)
    return dict(odd=(r & 1) == 1, hi2=(r & 2) != 0, hi4=(r & 4) != 0)


def _decay_products(f, m):
    w = f.shape[-1]
    f3 = f.reshape(NGROUP, GROUP, w)
    odd, hi2, hi4 = m["odd"], m["hi2"], m["hi4"]
    on_rows = lambda x, i: jnp.broadcast_to(x[:, i:i + 1, :], x.shape)
    p2 = f3 * jnp.where(odd, pltpu.roll(f3, 1, axis=1), 1.0)
    p4 = p2 * jnp.where(hi2, jnp.where(hi4, on_rows(p2, 5), on_rows(p2, 1)), 1.0)
    p8 = p4 * jnp.where(hi4, on_rows(p4, 3), 1.0)
    s2 = jnp.where(odd, 1.0, pltpu.roll(f3, GROUP - 1, axis=1))
    s4 = s2 * jnp.where(hi2, 1.0, jnp.where(hi4, on_rows(p2, 7), on_rows(p2, 3)))
    s8 = s4 * jnp.where(hi4, 1.0, on_rows(p4, 7))
    fac = {1: f3, 2: jnp.where(hi2, p2, s2), 4: jnp.where(hi4, p4, s4)}
    fac = {b: e.reshape(CHUNK, w) for b, e in fac.items()}

    tot = [p8[g, GROUP - 1:GROUP, :] for g in range(NGROUP)]

    def prefix_scaled(lo, hi):
        out, run = [p8[lo]], None
        for g in range(lo + 1, hi):
            run = tot[g - 1] if run is None else run * tot[g - 1]
            out.append(p8[g] * run)
        return out, (tot[hi - 1] if run is None else run * tot[hi - 1])

    def suffix_scaled(lo, hi):
        out, run = [s8[hi - 1]], None
        for g in range(hi - 2, lo - 1, -1):
            run = tot[g + 1] if run is None else run * tot[g + 1]
            out.append(s8[g] * run)
        return out[::-1]

    b = GROUP
    while b < CHUNK:
        gpb = b // GROUP
        pieces = []
        for blk in range(NGROUP // gpb):
            lo, hi = blk * gpb, (blk + 1) * gpb
            pieces += prefix_scaled(lo, hi)[0] if blk % 2 else suffix_scaled(lo, hi)
        fac[b] = jnp.concatenate(pieces, axis=0)
        b *= 2
    inc, total = prefix_scaled(0, NGROUP)
    exc = suffix_scaled(0, NGROUP)
    return fac, jnp.concatenate(inc, axis=0), jnp.concatenate(exc, axis=0), total


def _pair_operand(q, k, e, log2_b, upper_rows):
    b = 1 << log2_b
    if b >= GROUP:
        gpb = b // GROUP
        qk = jnp.concatenate([((q if (g // gpb) % 2 else k)[GROUP * g:GROUP * (g + 1)])
                              for g in range(NGROUP)], axis=0)
    elif b == 1:
        return jnp.where(upper_rows, q * e, k).astype(BF16)
    else:
        qk = jnp.where(upper_rows, q, k)
    return (qk * e).astype(BF16)


def _hgrn_body(q_ref, f_ref, v_ref, zg_ref, gn_ref, o_ref, s_scr, *, tb, hb):
    nchunk = tb // CHUNK
    nlev = int(math.log2(CHUNK))
    pair_w = 2 * LANES

    @pl.when(pl.program_id(2) == 0)
    def _():
        s_scr[...] = jnp.zeros(s_scr.shape, F32)

    row = lax.broadcasted_iota(jnp.int32, (CHUNK, 2 * CHUNK), 0)
    col = lax.broadcasted_iota(jnp.int32, (CHUNK, 2 * CHUNK), 1) & (CHUNK - 1)
    level_masks = [_pair_mask(row, col, lb) for lb in range(nlev)]
    trow = lax.broadcasted_iota(jnp.int32, (CHUNK, pair_w), 0)
    upper_rows = [(jnp.right_shift(trow, lb) & 1) == 1 for lb in range(int(math.log2(GROUP)))]
    group_masks = _group_row_masks(pair_w)
    gn = gn_ref[...]

    probs = [(hp, ci) for hp in range(hb // 2) for ci in range(nchunk)]
    pr = []
    for hp, ci in probs:
        d = dict(ws=slice(pair_w * hp, pair_w * (hp + 1)), rs=slice(CHUNK * ci, CHUNK * (ci + 1)))
        f = f_ref[0, d["rs"], d["ws"]]
        q = q_ref[0, d["rs"], d["ws"]].astype(F32)
        k = 1.0 - f
        d["v_b"] = v_ref[0, d["rs"], d["ws"]]
        fac, inc, exc, d["s_decay"] = _decay_products(f, group_masks)
        d["q_b"], d["k_b"] = q.astype(BF16), k.astype(BF16)
        d["ops"] = [_pair_operand(q, k, fac[1 << lb], lb, upper_rows[lb] if lb < len(upper_rows) else None)
                    for lb in range(nlev)]
        d["q_inc"] = (q * inc).astype(BF16)
        d["k_exc"] = (k * exc).astype(BF16)
        pr.append(d)
    for d in pr:
        d["attn"] = jnp.where(row == col, _mm_nt(d["q_b"], _blockdiag2(d["k_b"])), 0.0)
    for lb in range(nlev):
        for d in pr:
            p = d["ops"][lb]
            d["attn"] = jnp.where(level_masks[lb], _mm_nt(p, _blockdiag2(p)), d["attn"])
    for d in pr:
        d["o_intra"] = _mm(d["attn"], _blockdiag2(d["v_b"]))
        d["kv"] = [_mm_tn(d["v_b"][:, LANES * i:LANES * (i + 1)], d["k_exc"][:, LANES * i:LANES * (i + 1)])
                   for i in range(2)]

    for hp in range(hb // 2):
        states = [s_scr[2 * hp + i] for i in range(2)]
        for d, (php, _) in zip(pr, probs):
            if php != hp:
                continue
            for i in range(2):
                ls = slice(LANES * i, LANES * (i + 1))
                o = d["o_intra"][:, ls] + _mm_nt(d["q_inc"][:, ls], states[i])
                states[i] = states[i] * d["s_decay"][:, ls] + d["kv"][i]
                o = o * lax.rsqrt(jnp.mean(o * o, axis=-1, keepdims=True) + NORM_EPS) * gn
                os = slice(pair_w * hp + LANES * i, pair_w * hp + LANES * (i + 1))
                o_ref[0, d["rs"], os] = (o * zg_ref[0, d["rs"], os].astype(F32)).astype(BF16)
        for i in range(2):
            s_scr[2 * hp + i] = states[i]


def _hgrn_call(q, f, v, zg, gn, *, tb, hb):
    bsz, s, w = q.shape
    blk = lambda: pl.BlockSpec((1, tb, hb * LANES), lambda b, g, t: (b, t, g))
    return pl.pallas_call(
        functools.partial(_hgrn_body, tb=tb, hb=hb),
        grid=(bsz, w // (hb * LANES), s // tb),
        in_specs=[blk(), blk(), blk(), blk(), _const_spec(gn.shape)],
        out_specs=blk(),
        out_shape=jax.ShapeDtypeStruct((bsz, s, w), BF16),
        scratch_shapes=[pltpu.VMEM((hb, HEAD_DIM, HEAD_DIM), F32)],
        compiler_params=pltpu.CompilerParams(dimension_semantics=("arbitrary", "arbitrary", "arbitrary"),
                                             vmem_limit_bytes=VMEM_LIMIT),
        name="hgrn2_recurrence",
    )(q, f, v, zg, gn)


def kernel(x, p, w_in_even, conv_a_w, conv_b_w, a_log, dt_bias, gdn_norm_g, w_out_even, w_in_odd,
           lower_bounds, hgrn_norm_g, w_out_odd, ln_g, ln_b, w_pl, w_pl_gate):
    bsz, s, d = x.shape
    n = bsz * s
    assert d == D_MODEL and p.shape == (DEPTH, bsz, s, PL_DIM)
    assert s % max(IN_ROWS, GDN_ROWS, HGRN_ROWS) == 0 and n % OUT_ROWS == 0 and bsz % GDN_BATCH_ROWS == 0
    zb_end = 4 * CONV_A_WIDTH + 4 * GDN_WIDTH

    w_main = w_in_even[0, :, :zb_end].astype(BF16)
    wba = jnp.pad(w_in_even[0, :, zb_end:].astype(BF16), ((0, 0), (0, LANES - 2 * GDN_HEADS)))
    ya, q, k, v, zg, ba = _even_in_call(x, w_main, wba, conv_a_w[0], conv_b_w[0], tm=IN_ROWS)

    pad_lane = lambda a: jnp.pad(a, (GDN_HEADS, LANES - 2 * GDN_HEADS))
    arow = jnp.stack([pad_lane(a_log[0]), pad_lane(dt_bias[0])], axis=0)
    acol = jnp.stack([jnp.pad(a_log[0], (GDN_HEADS, 0)), jnp.pad(dt_bias[0], (GDN_HEADS, 0))], axis=1)
    og = _gdn_call(q, k, v, zg, ba, arow, acol, gdn_norm_g[0][None, :], tb=GDN_ROWS, bb=GDN_BATCH_ROWS)

    w_out = w_out_even[0].astype(BF16)
    p_rows = p.reshape(DEPTH, n, PL_DIM)
    x1 = _out_call([ya.reshape(n, -1), og.reshape(n, -1)], x.reshape(n, d), p_rows, 0,
                   [w_out[:CONV_A_WIDTH], w_out[CONV_A_WIDTH:]], w_pl_gate[0].astype(BF16),
                   w_pl[0].astype(BF16), ln_g[0][None, :], ln_b[0][None, :], tm=OUT_ROWS)

    hq, hf, hv, hzg = _odd_in_call(x1, w_in_odd[0].astype(BF16), lower_bounds, tm=IN_ROWS)
    shp = (bsz, s, HGRN_WIDTH)
    ho = _hgrn_call(hq.reshape(shp), hf.reshape(shp), hv.reshape(shp), hzg.reshape(shp),
                    hgrn_norm_g[0][None, :], tb=HGRN_ROWS, hb=HGRN_STEP_HEADS)
    out = _out_call([ho.reshape(n, -1)], x1, p_rows, 1, [w_out_odd[0].astype(BF16)],
                    w_pl_gate[1].astype(BF16), w_pl[1].astype(BF16), ln_g[1][None, :], ln_b[1][None, :],
                    tm=OUT_ROWS)
    return out.reshape(bsz, s, d)
```

```python
import functools
import math

import jax
import jax.numpy as jnp
from jax import lax
from jax.experimental import pallas as pl
from jax.experimental.pallas import tpu as pltpu

F32 = jnp.float32
BF16 = jnp.bfloat16

D_MODEL = 1024
DEPTH = 2
PL_DIM = 256
CONV_A_WIDTH = 1024
CONV_A_KERNEL = 3
GDN_HEADS = 8
HEAD_DIM = 128
GDN_WIDTH = GDN_HEADS * HEAD_DIM
GDN_CONV_KERNEL = 4
HGRN_WIDTH = 2 * D_MODEL
HGRN_HEADS = HGRN_WIDTH // HEAD_DIM
DEEPNORM_ALPHA = (2.0 * DEPTH) ** 0.25
NORM_EPS = 1e-5
L2_EPS = 1e-6

LANES = 128
SUBLANES = 8
MXU_COLS = 256
OUT_SUB_ROWS = 128
PIPE_LAG = 2
CHUNK = 64
HALO = SUBLANES
NEG_BIG = -1e30

VMEM_CAPACITY = 64 * 1024 * 1024
VMEM_LIMIT = VMEM_CAPACITY * 7 // 8

IN_ROWS = 256
OUT_ROWS = 512
GDN_ROWS, GDN_BATCH_ROWS = 256, 2
HGRN_ROWS, HGRN_STEP_HEADS = 512, 8


def _sigmoid(x):
    return 1.0 / (1.0 + jnp.exp(-x))


def _silu(x):
    return x * _sigmoid(x)


def _softplus(x):
    return jnp.maximum(x, 0.0) + jnp.log1p(jnp.exp(-jnp.abs(x)))


def _mm(a, b):
    return jnp.dot(a.astype(BF16), b.astype(BF16), preferred_element_type=F32)


def _mm_nt(a, b):
    return lax.dot_general(a.astype(BF16), b.astype(BF16), (((1,), (1,)), ((), ())),
                           preferred_element_type=F32)


def _mm_tn(a, b):
    return lax.dot_general(a.astype(BF16), b.astype(BF16), (((0,), (0,)), ((), ())),
                           preferred_element_type=F32)


def _mm_f32(a, b):
    return jnp.dot(a, b, preferred_element_type=F32, precision=lax.Precision.HIGHEST)


def _const_spec(shape):
    nd = len(shape)
    return pl.BlockSpec(shape, lambda *_: (0,) * nd, pipeline_mode=pl.Buffered(1))


def _pair_mask(row, col, log2_b):
    return (jnp.right_shift(jnp.bitwise_xor(row, col), log2_b) == 1) & (row > col)


def _blockdiag2(a):
    z = jnp.zeros((a.shape[0], LANES), a.dtype)
    return jnp.concatenate([jnp.concatenate([a[:, :LANES], z], axis=1),
                            jnp.concatenate([z, a[:, LANES:]], axis=1)], axis=0)


def _even_in_body(x_ref, w_ref, wba_ref, ca_ref, cb_ref,
                  ya_ref, q_ref, k_ref, v_ref, zg_ref, ba_ref, ua_scr, qkv_scr, *, tm):
    @pl.when(pl.program_id(1) == 0)
    def _():
        ua_scr[0:HALO, :] = jnp.zeros((HALO, CONV_A_WIDTH), F32)
        qkv_scr[0:HALO, :] = jnp.zeros((HALO, 3 * GDN_WIDTH), F32)

    xt = x_ref[0].astype(BF16)
    proj = lambda lo: jnp.dot(xt, w_ref[:, lo:lo + MXU_COLS], preferred_element_type=F32)
    qkv_lo, zb_lo = 4 * CONV_A_WIDTH, 4 * CONV_A_WIDTH + 3 * GDN_WIDTH

    def conv_taps(scr, w_ref, cur, sl):
        taps = w_ref.shape[0]
        scr[HALO:HALO + tm, sl] = cur
        ext = scr[0:HALO + tm, sl]
        acc = w_ref[taps - 1:taps, sl] * cur
        for j in range(1, taps):
            acc = acc + w_ref[taps - 1 - j:taps - j, sl] * pltpu.roll(ext, j, axis=0)[HALO:]
        scr[0:HALO, sl] = scr[tm:tm + HALO, sl]
        return acc

    tasks = []

    def mixer_a(g):
        sl = slice(MXU_COLS * g, MXU_COLS * (g + 1))
        def mm():
            return [proj(part * CONV_A_WIDTH + MXU_COLS * g) for part in range(4)]
        def ew(r):
            h, c, b, z = r
            ya_ref[0, :, sl] = (b * conv_taps(ua_scr, ca_ref, c * h, sl) * _silu(z)).astype(BF16)
        return mm, ew

    def mixer_b_in(g):
        sl = slice(MXU_COLS * g, MXU_COLS * (g + 1))
        def mm():
            return proj(qkv_lo + MXU_COLS * g)
        def ew(r):
            a = _silu(conv_taps(qkv_scr, cb_ref, r, sl))
            for i in range(MXU_COLS // LANES):
                part, hj = divmod(g * (MXU_COLS // LANES) + i, GDN_HEADS)
                hs = slice(LANES * hj, LANES * (hj + 1))
                ah = a[:, LANES * i:LANES * (i + 1)]
                if part < 2:
                    ah = ah * lax.rsqrt(jnp.sum(ah * ah, axis=-1, keepdims=True) + L2_EPS)
                (q_ref, k_ref, v_ref)[part][0, :, hs] = ah.astype(BF16)
        return mm, ew

    def gate_b(g):
        sl = slice(MXU_COLS * g, MXU_COLS * (g + 1))
        def mm():
            return proj(zb_lo + MXU_COLS * g)
        def ew(r):
            zg_ref[0, :, sl] = _silu(r).astype(BF16)
        return mm, ew

    n_a, n_b = CONV_A_WIDTH // MXU_COLS, 3 * GDN_WIDTH // MXU_COLS
    for i in range(n_a):
        tasks.append(mixer_a(i))
        tasks += [mixer_b_in(g) for g in range(i * n_b // n_a, (i + 1) * n_b // n_a)]
        tasks.append(gate_b(i))
    for mm, ew in tasks:
        ew(mm())
    ba_ref[0] = jnp.dot(xt, wba_ref[...], preferred_element_type=F32)


def _even_in_call(x, w, wba, ca, cb, *, tm):
    bsz, s, _ = x.shape
    row = lambda w: pl.BlockSpec((1, tm, w), lambda b, t: (b, t, 0))
    return pl.pallas_call(
        functools.partial(_even_in_body, tm=tm),
        grid=(bsz, s // tm),
        in_specs=[row(D_MODEL), _const_spec(w.shape), _const_spec(wba.shape), _const_spec(ca.shape),
                  _const_spec(cb.shape)],
        out_specs=[row(CONV_A_WIDTH), row(GDN_WIDTH), row(GDN_WIDTH), row(GDN_WIDTH), row(GDN_WIDTH),
                   row(LANES)],
        out_shape=[jax.ShapeDtypeStruct((bsz, s, CONV_A_WIDTH), BF16),
                   jax.ShapeDtypeStruct((bsz, s, GDN_WIDTH), BF16),
                   jax.ShapeDtypeStruct((bsz, s, GDN_WIDTH), BF16),
                   jax.ShapeDtypeStruct((bsz, s, GDN_WIDTH), BF16),
                   jax.ShapeDtypeStruct((bsz, s, GDN_WIDTH), BF16),
                   jax.ShapeDtypeStruct((bsz, s, LANES), F32)],
        scratch_shapes=[pltpu.VMEM((tm + HALO, CONV_A_WIDTH), F32),
                        pltpu.VMEM((tm + HALO, 3 * GDN_WIDTH), F32)],
        compiler_params=pltpu.CompilerParams(dimension_semantics=("arbitrary", "arbitrary"),
                                             vmem_limit_bytes=VMEM_LIMIT),
        name="even_in_proj",
    )(x, w, wba, ca, cb)


def _gdn_body(q_ref, k_ref, v_ref, zg_ref, ba_ref, arow_ref, acol_ref, gn_ref,
              o_ref, s_scr, *, tb, bb):
    nchunk = tb // CHUNK

    @pl.when(pl.program_id(1) == 0)
    def _():
        s_scr[...] = jnp.zeros(s_scr.shape, F32)

    ti = lax.broadcasted_iota(jnp.int32, (tb, tb), 0)
    tj = lax.broadcasted_iota(jnp.int32, (tb, tb), 1)
    same_chunk = (ti // CHUNK) == (tj // CHUNK)
    lower_blk = jnp.where(same_chunk & (tj <= ti), 1.0, 0.0).astype(F32)
    upper_blk = jnp.where(same_chunk & (ti <= tj), 1.0, 0.0).astype(F32)
    beta_cols, gc_cols, gc_rows = [], [], []
    for bi in range(bb):
        ba = ba_ref[bi]
        beta_cols.append(_sigmoid(ba))
        g_cols = -jnp.exp(arow_ref[0:1, :]) * _softplus(ba + arow_ref[1:2, :])
        ba_rows = jnp.transpose(ba)[0:2 * GDN_HEADS, :]
        g_rows = -jnp.exp(acol_ref[:, 0:1]) * _softplus(ba_rows + acol_ref[:, 1:2])
        gc_cols.append(_mm_f32(lower_blk, g_cols))
        gc_rows.append(_mm_f32(g_rows, upper_blk))

    pair_w = 2 * LANES
    row = lax.broadcasted_iota(jnp.int32, (CHUNK, 2 * CHUNK), 0)
    lane = lax.broadcasted_iota(jnp.int32, (CHUNK, 2 * CHUNK), 1)
    col = lane & (CHUNK - 1)
    first = lane < CHUNK
    eye = jnp.where(row == col, 1.0, 0.0).astype(F32)
    level_masks = [_pair_mask(row, col, lb) for lb in range(int(math.log2(CHUNK)))]
    lane1 = lax.broadcasted_iota(jnp.int32, (1, LANES), 1)
    scale = HEAD_DIM ** -0.5
    gn = gn_ref[...]

    def blockdiag_c(a, keep=None):
        top = first if keep is None else keep & first
        bot = ~first if keep is None else keep & ~first
        return jnp.concatenate([jnp.where(top, a, 0.0), jnp.where(bot, a, 0.0)], axis=0).astype(BF16)

    def per_head(c0, c1):
        return jnp.concatenate([jnp.broadcast_to(c0, (CHUNK, LANES)), jnp.broadcast_to(c1, (CHUNK, LANES))],
                               axis=1)

    probs = [(bi, hp, ci) for bi in range(bb) for hp in range(GDN_HEADS // 2) for ci in range(nchunk)]
    pr = []
    for bi, hp, ci in probs:
        h0, h1 = 2 * hp, 2 * hp + 1
        rs = slice(CHUNK * ci, CHUNK * (ci + 1))
        ws = slice(pair_w * hp, pair_w * (hp + 1))
        d = dict(bi=bi, hp=hp, rs=rs)
        q = q_ref[bi, rs, ws].astype(F32) * scale
        d["k_b"] = k_ref[bi, rs, ws]
        k = d["k_b"].astype(F32)
        v = v_ref[bi, rs, ws].astype(F32)
        gcc = [gc_cols[bi][rs, GDN_HEADS + h:GDN_HEADS + h + 1] for h in (h0, h1)]
        g_last = [g[CHUNK - 1:CHUNK, :] for g in gcc]
        tile = slice(LANES * (ci // 2), LANES * (ci // 2 + 1))
        r0 = gc_rows[bi][GDN_HEADS + h0:GDN_HEADS + h0 + 1, tile]
        r1 = gc_rows[bi][GDN_HEADS + h1:GDN_HEADS + h1 + 1, tile]
        if ci % 2:
            r0 = pltpu.roll(r0, CHUNK, axis=1)
        else:
            r1 = pltpu.roll(r1, CHUNK, axis=1)
        gcr = jnp.where(lane1 < CHUNK, r0, r1)
        gcc2 = jnp.where(first, gcc[0], gcc[1])
        d["decay"] = jnp.exp(jnp.where(row >= col, gcc2 - gcr, NEG_BIG))
        beta = per_head(beta_cols[bi][rs, h0:h0 + 1], beta_cols[bi][rs, h1:h1 + 1])
        e_gc = per_head(jnp.exp(gcc[0]), jnp.exp(gcc[1]))
        kb = k * beta
        d["kbq"] = jnp.concatenate([kb, q], axis=0).astype(BF16)
        vb, kbe = (v * beta).astype(BF16), (kb * e_gc).astype(BF16)
        z = jnp.zeros((CHUNK, pair_w), BF16)
        d["rhs"] = jnp.concatenate(
            [jnp.concatenate([vb[:, :LANES], kbe[:, :LANES], z], axis=1),
             jnp.concatenate([z, vb[:, LANES:], kbe[:, LANES:]], axis=1)], axis=0)
        d["q_dec"] = (q * e_gc).astype(BF16)
        d["k_dec"] = (k * per_head(jnp.exp(g_last[0] - gcc[0]), jnp.exp(g_last[1] - gcc[1]))).astype(BF16)
        d["s_decay"] = [jnp.exp(g) for g in g_last]
        pr.append(d)

    for d in pr:
        both = _mm_nt(d["kbq"], _blockdiag2(d["k_b"]))
        d["low"] = jnp.where(row > col, both[:CHUNK] * d["decay"], 0.0)
        d["attn"] = (both[CHUNK:] * d["decay"]).astype(BF16)
        d["tinv"] = eye - jnp.where(level_masks[0], d["low"], 0.0)
    for lb in range(1, len(level_masks)):
        for d in pr:
            d["p"] = _mm(d["tinv"], blockdiag_c(d["low"], level_masks[lb]))
        for d in pr:
            d["tinv"] = d["tinv"] - _mm(d["p"], blockdiag_c(d["tinv"]))
    for d in pr:
        d["uw"] = _mm(d["tinv"], d["rhs"])

    states = [s_scr[i] for i in range(bb * GDN_HEADS)]
    for ci in range(nchunk):
        cur = [d for d, (_, _, c) in zip(pr, probs) if c == ci]
        for d in cur:
            d["wq_s"] = []
            for i in range(2):
                ls = slice(LANES * i, LANES * (i + 1))
                w = d["uw"][:, pair_w * i + LANES:pair_w * (i + 1)].astype(BF16)
                d["wq_s"].append(_mm(jnp.concatenate([w, d["q_dec"][:, ls]], axis=0),
                                     states[d["bi"] * GDN_HEADS + 2 * d["hp"] + i]))
        for d in cur:
            v_new = jnp.concatenate([d["uw"][:, pair_w * i:pair_w * i + LANES] - d["wq_s"][i][:CHUNK]
                                     for i in range(2)], axis=1).astype(BF16)
            o_pair = (jnp.concatenate([d["wq_s"][i][CHUNK:] for i in range(2)], axis=1)
                      + _mm(d["attn"], _blockdiag2(v_new)))
            for i in range(2):
                h = 2 * d["hp"] + i
                si = d["bi"] * GDN_HEADS + h
                ls = slice(LANES * i, LANES * (i + 1))
                hs = slice(LANES * h, LANES * (h + 1))
                states[si] = states[si] * d["s_decay"][i] + _mm_tn(d["k_dec"][:, ls], v_new[:, ls])
                o = o_pair[:, ls]
                o = o * lax.rsqrt(jnp.mean(o * o, axis=-1, keepdims=True) + NORM_EPS) * gn
                o_ref[d["bi"], d["rs"], hs] = (o * zg_ref[d["bi"], d["rs"], hs].astype(F32)).astype(BF16)
    for i in range(bb * GDN_HEADS):
        s_scr[i] = states[i]


def _gdn_call(q, k, v, zg, ba, arow, acol, gn, *, tb, bb):
    bsz, s, _ = q.shape
    row = lambda w: pl.BlockSpec((bb, tb, w), lambda b, t: (b, t, 0))
    return pl.pallas_call(
        functools.partial(_gdn_body, tb=tb, bb=bb),
        grid=(bsz // bb, s // tb),
        in_specs=[row(GDN_WIDTH), row(GDN_WIDTH), row(GDN_WIDTH), row(GDN_WIDTH), row(LANES),
                  _const_spec(arow.shape), _const_spec(acol.shape), _const_spec(gn.shape)],
        out_specs=row(GDN_WIDTH),
        out_shape=jax.ShapeDtypeStruct((bsz, s, GDN_WIDTH), BF16),
        scratch_shapes=[pltpu.VMEM((bb * GDN_HEADS, HEAD_DIM, HEAD_DIM), F32)],
        compiler_params=pltpu.CompilerParams(dimension_semantics=("arbitrary", "arbitrary"),
                                             vmem_limit_bytes=VMEM_LIMIT),
        name="gated_delta_rule",
    )(q, k, v, zg, ba, arow, acol, gn)


def _out_body(*refs, n_y):
    y_refs, (x_ref, p_ref) = refs[:n_y], refs[n_y:n_y + 2]
    wo_refs = refs[n_y + 2:2 * n_y + 2]
    wg_ref, wp_ref, lg_ref, lb_ref, o_ref = refs[2 * n_y + 2:]
    tm = x_ref.shape[0]
    subs = [slice(r, r + OUT_SUB_ROWS) for r in range(0, tm, OUT_SUB_ROWS)]
    s_parts, embs, xns, gates = [], [], [], []
    for rs in subs:
        s = jnp.dot(y_refs[0][rs, :], wo_refs[0][...], preferred_element_type=F32)
        for y_ref, wo_ref in zip(y_refs[1:], wo_refs[1:]):
            s = s + jnp.dot(y_ref[rs, :], wo_ref[...], preferred_element_type=F32)
        s_parts.append(s)
    for rs in subs:
        embs.append(jnp.dot(p_ref[rs, :].astype(BF16), wp_ref[...], preferred_element_type=F32))
    for rs, s in zip(subs, s_parts):
        t = DEEPNORM_ALPHA * x_ref[rs, :] + s
        mu = jnp.mean(t, axis=-1, keepdims=True)
        tc = t - mu
        var = jnp.mean(tc * tc, axis=-1, keepdims=True)
        xn = tc * lax.rsqrt(var + NORM_EPS) * lg_ref[...] + lb_ref[...]
        xns.append(xn)
        gates.append(jnp.dot(xn.astype(BF16), wg_ref[...], preferred_element_type=F32))
    for rs, xn, emb, g in zip(subs, xns, embs, gates):
        o_ref[rs, :] = xn + emb * _sigmoid(g)


def _out_call(ys, x, p, layer, wos, wg, wp, lg, lb, *, tm):
    n, d = x.shape
    row = lambda w: pl.BlockSpec((tm, w), lambda i: (i, 0))
    p_spec = pl.BlockSpec((None, tm, p.shape[2]), lambda i: (layer, i, 0))
    return pl.pallas_call(
        functools.partial(_out_body, n_y=len(ys)),
        grid=(n // tm,),
        in_specs=([row(y.shape[1]) for y in ys] + [row(d), p_spec]
                  + [_const_spec(w.shape) for w in wos]
                  + [_const_spec(wg.shape), _const_spec(wp.shape), _const_spec(lg.shape), _const_spec(lb.shape)]),
        out_specs=row(d),
        out_shape=jax.ShapeDtypeStruct((n, d), F32),
        compiler_params=pltpu.CompilerParams(dimension_semantics=("arbitrary",),
                                             vmem_limit_bytes=VMEM_LIMIT),
        name="out_proj_norm_gate",
    )(*ys, x, p, *wos, wg, wp, lg, lb)


def _odd_in_body(x_ref, w_ref, lbr_ref, q_ref, f_ref, v_ref, zg_ref):
    xt = x_ref[...].astype(BF16)
    lb_raw = lbr_ref[...]
    e = jnp.exp(lb_raw - jnp.max(lb_raw, axis=0, keepdims=True))
    sm = e / jnp.sum(e, axis=0, keepdims=True)
    lower = (sm[0:1, :] + sm[1:2, :]) - sm[0:1, :]
    w = HGRN_WIDTH
    q_ref[...] = _silu(jnp.dot(xt, w_ref[:, 0:w], preferred_element_type=F32)).astype(BF16)
    f_raw = jnp.dot(xt, w_ref[:, w:2 * w], preferred_element_type=F32)
    f_ref[...] = lower + (1.0 - lower) * _sigmoid(f_raw)
    v_ref[...] = jnp.dot(xt, w_ref[:, 2 * w:3 * w], preferred_element_type=F32).astype(BF16)
    zg_ref[...] = _silu(jnp.dot(xt, w_ref[:, 3 * w:4 * w], preferred_element_type=F32)).astype(BF16)


def _odd_in_call(x, w, lbr, *, tm):
    n, d = x.shape
    row = lambda wd: pl.BlockSpec((tm, wd), lambda i: (i, 0))
    return pl.pallas_call(
        _odd_in_body,
        grid=(n // tm,),
        in_specs=[row(d), _const_spec(w.shape), _const_spec(lbr.shape)],
        out_specs=[row(HGRN_WIDTH)] * 4,
        out_shape=[jax.ShapeDtypeStruct((n, HGRN_WIDTH), BF16),
                   jax.ShapeDtypeStruct((n, HGRN_WIDTH), F32),
                   jax.ShapeDtypeStruct((n, HGRN_WIDTH), BF16),
                   jax.ShapeDtypeStruct((n, HGRN_WIDTH), BF16)],
        compiler_params=pltpu.CompilerParams(dimension_semantics=("arbitrary",),
                                             vmem_limit_bytes=VMEM_LIMIT),
        name="odd_in_proj",
    )(x, w, lbr)


GROUP = SUBLANES
NGROUP = CHUNK // GROUP


def _group_row_masks(w):
    r = lax.broadcasted_iota(jnp.int32, (NGROUP, GROUP, w), 1)
    return dict(odd=(r & 1) == 1, hi2=(r & 2) != 0, hi4=(r & 4) != 0)


def _decay_products(f, m):
    w = f.shape[-1]
    f3 = f.reshape(NGROUP, GROUP, w)
    odd, hi2, hi4 = m["odd"], m["hi2"], m["hi4"]
    on_rows = lambda x, i: jnp.broadcast_to(x[:, i:i + 1, :], x.shape)
    p2 = f3 * jnp.where(odd, pltpu.roll(f3, 1, axis=1), 1.0)
    p4 = p2 * jnp.where(hi2, jnp.where(hi4, on_rows(p2, 5), on_rows(p2, 1)), 1.0)
    p8 = p4 * jnp.where(hi4, on_rows(p4, 3), 1.0)
    s2 = jnp.where(odd, 1.0, pltpu.roll(f3, GROUP - 1, axis=1))
    s4 = s2 * jnp.where(hi2, 1.0, jnp.where(hi4, on_rows(p2, 7), on_rows(p2, 3)))
    s8 = s4 * jnp.where(hi4, 1.0, on_rows(p4, 7))
    fac = {1: f3, 2: jnp.where(hi2, p2, s2), 4: jnp.where(hi4, p4, s4)}
    return {b: e.reshape(CHUNK, w) for b, e in fac.items()}, p8, s8


def _small_operand(q, k, e, log2_b, upper_rows):
    if log2_b == 0:
        return jnp.where(upper_rows, q * e, k).astype(BF16)
    return (jnp.where(upper_rows, q, k) * e).astype(BF16)


def _group_operands(q, k, p8, s8):
    w = q.shape[-1]
    qp = q.reshape(NGROUP, GROUP, w) * p8
    ks = k.reshape(NGROUP, GROUP, w) * s8
    tot = [p8[g, GROUP - 1:GROUP, :] for g in range(NGROUP)]

    def prefix_scaled(lo, hi):
        out, run = [qp[lo]], None
        for g in range(lo + 1, hi):
            run = tot[g - 1] if run is None else run * tot[g - 1]
            out.append(qp[g] * run)
        return out, (tot[hi - 1] if run is None else run * tot[hi - 1])

    def suffix_scaled(lo, hi):
        out, run = [ks[hi - 1]], None
        for g in range(hi - 2, lo - 1, -1):
            run = tot[g + 1] if run is None else run * tot[g + 1]
            out.append(ks[g] * run)
        return out[::-1]

    ops, b = [], GROUP
    while b < CHUNK:
        gpb = b // GROUP
        pieces = []
        for blk in range(NGROUP // gpb):
            lo, hi = blk * gpb, (blk + 1) * gpb
            pieces += prefix_scaled(lo, hi)[0] if blk % 2 else suffix_scaled(lo, hi)
        ops.append(jnp.concatenate(pieces, axis=0).astype(BF16))
        b *= 2
    q_inc, total = prefix_scaled(0, NGROUP)
    k_exc = suffix_scaled(0, NGROUP)
    return ops, jnp.concatenate(q_inc, axis=0).astype(BF16), jnp.concatenate(k_exc, axis=0).astype(BF16), total


def _hgrn_body(q_ref, f_ref, v_ref, zg_ref, gn_ref, o_ref, s_scr, *, tb, hb):
    nchunk = tb // CHUNK
    nlev = int(math.log2(CHUNK))
    pair_w = 2 * LANES

    @pl.when(pl.program_id(2) == 0)
    def _():
        s_scr[...] = jnp.zeros(s_scr.shape, F32)

    row = lax.broadcasted_iota(jnp.int32, (CHUNK, 2 * CHUNK), 0)
    col = lax.broadcasted_iota(jnp.int32, (CHUNK, 2 * CHUNK), 1) & (CHUNK - 1)
    level_masks = [_pair_mask(row, col, lb) for lb in range(nlev)]
    trow = lax.broadcasted_iota(jnp.int32, (CHUNK, pair_w), 0)
    upper_rows = [(jnp.right_shift(trow, lb) & 1) == 1 for lb in range(int(math.log2(GROUP)))]
    group_masks = _group_row_masks(pair_w)
    gn = gn_ref[...]

    probs = [(hp, ci) for hp in range(hb // 2) for ci in range(nchunk)]
    states = {}

    def stage_a(hp, ci):
        d = dict(hp=hp, ws=slice(pair_w * hp, pair_w * (hp + 1)), rs=slice(CHUNK * ci, CHUNK * (ci + 1)))
        f = f_ref[0, d["rs"], d["ws"]]
        q = q_ref[0, d["rs"], d["ws"]].astype(F32)
        k = 1.0 - f
        d["v_b"] = v_ref[0, d["rs"], d["ws"]]
        fac, p8, s8 = _decay_products(f, group_masks)
        group_ops, d["q_inc"], k_exc, d["s_decay"] = _group_operands(q, k, p8, s8)
        ops = [_small_operand(q, k, fac[1 << lb], lb, upper_rows[lb]) for lb in range(len(upper_rows))]
        q_b, k_b = q.astype(BF16), k.astype(BF16)
        d["scores"] = [_mm_nt(q_b, _blockdiag2(k_b))]
        d["scores"] += [_mm_nt(p, _blockdiag2(p)) for p in ops + group_ops]
        d["kv"] = [_mm_tn(d["v_b"][:, LANES * i:LANES * (i + 1)], k_exc[:, LANES * i:LANES * (i + 1)])
                   for i in range(2)]
        return d

    def stage_b(d):
        attn = jnp.where(row == col, d["scores"][0], 0.0)
        for lb in range(nlev):
            attn = jnp.where(level_masks[lb], d["scores"][lb + 1], attn)
        d["o_intra"] = _mm(attn, _blockdiag2(d["v_b"]))

    def stage_c(d):
        for i in range(2):
            h = 2 * d["hp"] + i
            if h not in states:
                states[h] = s_scr[h]
            ls = slice(LANES * i, LANES * (i + 1))
            o = d["o_intra"][:, ls] + _mm_nt(d["q_inc"][:, ls], states[h])
            states[h] = states[h] * d["s_decay"][:, ls] + d["kv"][i]
            o = o * lax.rsqrt(jnp.mean(o * o, axis=-1, keepdims=True) + NORM_EPS) * gn
            os = slice(LANES * h, LANES * (h + 1))
            o_ref[0, d["rs"], os] = (o * zg_ref[0, d["rs"], os].astype(F32)).astype(BF16)

    done_a = []
    for n in range(len(probs) + 2 * PIPE_LAG):
        if n < len(probs):
            done_a.append(stage_a(*probs[n]))
        if 0 <= n - PIPE_LAG < len(probs):
            stage_b(done_a[n - PIPE_LAG])
        if 0 <= n - 2 * PIPE_LAG < len(probs):
            stage_c(done_a[n - 2 * PIPE_LAG])
    for h in range(hb):
        s_scr[h] = states[h]


def _hgrn_call(q, f, v, zg, gn, *, tb, hb):
    bsz, s, w = q.shape
    blk = lambda: pl.BlockSpec((1, tb, hb * LANES), lambda b, g, t: (b, t, g))
    return pl.pallas_call(
        functools.partial(_hgrn_body, tb=tb, hb=hb),
        grid=(bsz, w // (hb * LANES), s // tb),
        in_specs=[blk(), blk(), blk(), blk(), _const_spec(gn.shape)],
        out_specs=blk(),
        out_shape=jax.ShapeDtypeStruct((bsz, s, w), BF16),
        scratch_shapes=[pltpu.VMEM((hb, HEAD_DIM, HEAD_DIM), F32)],
        compiler_params=pltpu.CompilerParams(dimension_semantics=("arbitrary", "arbitrary", "arbitrary"),
                                             vmem_limit_bytes=VMEM_LIMIT),
        name="hgrn2_recurrence",
    )(q, f, v, zg, gn)


def kernel(x, p, w_in_even, conv_a_w, conv_b_w, a_log, dt_bias, gdn_norm_g, w_out_even, w_in_odd,
           lower_bounds, hgrn_norm_g, w_out_odd, ln_g, ln_b, w_pl, w_pl_gate):
    bsz, s, d = x.shape
    n = bsz * s
    assert d == D_MODEL and p.shape == (DEPTH, bsz, s, PL_DIM)
    assert s % max(IN_ROWS, GDN_ROWS, HGRN_ROWS) == 0 and n % OUT_ROWS == 0 and bsz % GDN_BATCH_ROWS == 0
    zb_end = 4 * CONV_A_WIDTH + 4 * GDN_WIDTH

    w_main = w_in_even[0, :, :zb_end].astype(BF16)
    wba = jnp.pad(w_in_even[0, :, zb_end:].astype(BF16), ((0, 0), (0, LANES - 2 * GDN_HEADS)))
    ya, q, k, v, zg, ba = _even_in_call(x, w_main, wba, conv_a_w[0], conv_b_w[0], tm=IN_ROWS)

    pad_lane = lambda a: jnp.pad(a, (GDN_HEADS, LANES - 2 * GDN_HEADS))
    arow = jnp.stack([pad_lane(a_log[0]), pad_lane(dt_bias[0])], axis=0)
    acol = jnp.stack([jnp.pad(a_log[0], (GDN_HEADS, 0)), jnp.pad(dt_bias[0], (GDN_HEADS, 0))], axis=1)
    og = _gdn_call(q, k, v, zg, ba, arow, acol, gdn_norm_g[0][None, :], tb=GDN_ROWS, bb=GDN_BATCH_ROWS)

    w_out = w_out_even[0].astype(BF16)
    p_rows = p.reshape(DEPTH, n, PL_DIM)
    x1 = _out_call([ya.reshape(n, -1), og.reshape(n, -1)], x.reshape(n, d), p_rows, 0,
                   [w_out[:CONV_A_WIDTH], w_out[CONV_A_WIDTH:]], w_pl_gate[0].astype(BF16),
                   w_pl[0].astype(BF16), ln_g[0][None, :], ln_b[0][None, :], tm=OUT_ROWS)

    hq, hf, hv, hzg = _odd_in_call(x1, w_in_odd[0].astype(BF16), lower_bounds, tm=IN_ROWS)
    shp = (bsz, s, HGRN_WIDTH)
    ho = _hgrn_call(hq.reshape(shp), hf.reshape(shp), hv.reshape(shp), hzg.reshape(shp),
                    hgrn_norm_g[0][None, :], tb=HGRN_ROWS, hb=HGRN_STEP_HEADS)
    out = _out_call([ho.reshape(n, -1)], x1, p_rows, 1, [w_out_odd[0].astype(BF16)],
                    w_pl_gate[1].astype(BF16), w_pl[1].astype(BF16), ln_g[1][None, :], ln_b[1][None, :],
                    tm=OUT_ROWS)
    return out.reshape(bsz, s, d)
```

```python
import functools
import math

import jax
import jax.numpy as jnp
from jax import lax
from jax.experimental import pallas as pl
from jax.experimental.pallas import tpu as pltpu

F32 = jnp.float32
BF16 = jnp.bfloat16

D_MODEL = 1024
DEPTH = 2
PL_DIM = 256
CONV_A_WIDTH = 1024
CONV_A_KERNEL = 3
GDN_HEADS = 8
HEAD_DIM = 128
GDN_WIDTH = GDN_HEADS * HEAD_DIM
GDN_CONV_KERNEL = 4
HGRN_WIDTH = 2 * D_MODEL
HGRN_HEADS = HGRN_WIDTH // HEAD_DIM
DEEPNORM_ALPHA = (2.0 * DEPTH) ** 0.25
NORM_EPS = 1e-5
L2_EPS = 1e-6

LANES = 128
SUBLANES = 8
MXU_COLS = 256
OUT_SUB_ROWS = 128
PIPE_LAG = 2
EW_ROWS = 64
CHUNK = 64
HALO = SUBLANES
NEG_BIG = -1e30

VMEM_CAPACITY = 64 * 1024 * 1024
VMEM_LIMIT = VMEM_CAPACITY * 7 // 8

IN_ROWS = 256
OUT_ROWS = 1024
GDN_ROWS, GDN_BATCH_ROWS = 256, 2
HGRN_ROWS, HGRN_STEP_HEADS = 512, 8


def _sigmoid(x):
    return 1.0 / (1.0 + jnp.exp(-x))


def _silu(x):
    return x * _sigmoid(x)


def _softplus(x):
    return jnp.maximum(x, 0.0) + jnp.log1p(jnp.exp(-jnp.abs(x)))


def _mm(a, b):
    return jnp.dot(a.astype(BF16), b.astype(BF16), preferred_element_type=F32)


def _mm_nt(a, b):
    return lax.dot_general(a.astype(BF16), b.astype(BF16), (((1,), (1,)), ((), ())),
                           preferred_element_type=F32)


def _mm_tn(a, b):
    return lax.dot_general(a.astype(BF16), b.astype(BF16), (((0,), (0,)), ((), ())),
                           preferred_element_type=F32)


def _mm_f32(a, b):
    return jnp.dot(a, b, preferred_element_type=F32, precision=lax.Precision.HIGHEST)


def _const_spec(shape):
    nd = len(shape)
    return pl.BlockSpec(shape, lambda *_: (0,) * nd, pipeline_mode=pl.Buffered(1))


def _pair_mask(row, col, log2_b):
    return (jnp.right_shift(jnp.bitwise_xor(row, col), log2_b) == 1) & (row > col)


def _blockdiag2(a):
    z = jnp.zeros((a.shape[0], LANES), a.dtype)
    return jnp.concatenate([jnp.concatenate([a[:, :LANES], z], axis=1),
                            jnp.concatenate([z, a[:, LANES:]], axis=1)], axis=0)


def _even_in_body(x_ref, w_ref, wba_ref, ca_ref, cb_ref,
                  ya_ref, q_ref, k_ref, v_ref, zg_ref, ba_ref, ua_scr, qkv_scr, *, tm):
    @pl.when(pl.program_id(1) == 0)
    def _():
        ua_scr[0:HALO, :] = jnp.zeros((HALO, CONV_A_WIDTH), F32)
        qkv_scr[0:HALO, :] = jnp.zeros((HALO, 3 * GDN_WIDTH), F32)

    xt = x_ref[0].astype(BF16)
    def proj(lo):
        return jnp.dot(xt, w_ref[:, lo:lo + MXU_COLS], preferred_element_type=F32)

    qkv_lo, zb_lo = 4 * CONV_A_WIDTH, 4 * CONV_A_WIDTH + 3 * GDN_WIDTH

    def conv_taps(scr, w_ref, cur, sl, r0):
        taps, nr = w_ref.shape[0], cur.shape[0]
        scr[HALO + r0:HALO + r0 + nr, sl] = cur
        ext = scr[r0:HALO + r0 + nr, sl]
        acc = w_ref[taps - 1:taps, sl] * cur
        for j in range(1, taps):
            acc = acc + w_ref[taps - 1 - j:taps - j, sl] * pltpu.roll(ext, j, axis=0)[HALO:]
        return acc

    def keep_tail(scr, sl):
        scr[0:HALO, sl] = scr[tm:tm + HALO, sl]

    rows = [slice(r, r + EW_ROWS) for r in range(0, tm, EW_ROWS)]

    def mixer_a(g):
        sl = slice(MXU_COLS * g, MXU_COLS * (g + 1))
        dots = [functools.partial(proj, part * CONV_A_WIDTH + MXU_COLS * g) for part in range(4)]
        def ew(res, rs):
            h, c, b, z = [r[rs] for r in res]
            ya_ref[0, rs, sl] = (b * conv_taps(ua_scr, ca_ref, c * h, sl, rs.start) * _silu(z)).astype(BF16)
        return dots, ew, functools.partial(keep_tail, ua_scr, sl)

    def mixer_b_in(g):
        sl = slice(MXU_COLS * g, MXU_COLS * (g + 1))
        dots = [functools.partial(proj, qkv_lo + MXU_COLS * g)]
        def ew(res, rs):
            a = _silu(conv_taps(qkv_scr, cb_ref, res[0][rs], sl, rs.start))
            for i in range(MXU_COLS // LANES):
                part, hj = divmod(g * (MXU_COLS // LANES) + i, GDN_HEADS)
                hs = slice(LANES * hj, LANES * (hj + 1))
                ah = a[:, LANES * i:LANES * (i + 1)]
                if part < 2:
                    ah = ah * lax.rsqrt(jnp.sum(ah * ah, axis=-1, keepdims=True) + L2_EPS)
                (q_ref, k_ref, v_ref)[part][0, rs, hs] = ah.astype(BF16)
        return dots, ew, functools.partial(keep_tail, qkv_scr, sl)

    def gate_b(g):
        sl = slice(MXU_COLS * g, MXU_COLS * (g + 1))
        dots = [functools.partial(proj, zb_lo + MXU_COLS * g)]
        def ew(res, rs):
            zg_ref[0, rs, sl] = _silu(res[0][rs]).astype(BF16)
        return dots, ew, lambda: None

    tasks = []
    n_a, n_b = CONV_A_WIDTH // MXU_COLS, 3 * GDN_WIDTH // MXU_COLS
    for i in range(n_a):
        tasks.append(mixer_a(i))
        tasks += [mixer_b_in(g) for g in range(i * n_b // n_a, (i + 1) * n_b // n_a)]
        tasks.append(gate_b(i))
    res = [d() for d in tasks[0][0]]
    for i, (_, ew, wrap_up) in enumerate(tasks):
        nxt_dots = tasks[i + 1][0] if i + 1 < len(tasks) else []
        nxt = []
        for j in range(max(len(nxt_dots), len(rows))):
            if j < len(nxt_dots):
                nxt.append(nxt_dots[j]())
            if j < len(rows):
                ew(res, rows[j])
        wrap_up()
        res = nxt
    ba_ref[0] = jnp.dot(xt, wba_ref[...], preferred_element_type=F32)


def _even_in_call(x, w, wba, ca, cb, *, tm):
    bsz, s, _ = x.shape
    row = lambda w: pl.BlockSpec((1, tm, w), lambda b, t: (b, t, 0))
    return pl.pallas_call(
        functools.partial(_even_in_body, tm=tm),
        grid=(bsz, s // tm),
        in_specs=[row(D_MODEL), _const_spec(w.shape), _const_spec(wba.shape), _const_spec(ca.shape),
                  _const_spec(cb.shape)],
        out_specs=[row(CONV_A_WIDTH), row(GDN_WIDTH), row(GDN_WIDTH), row(GDN_WIDTH), row(GDN_WIDTH),
                   row(LANES)],
        out_shape=[jax.ShapeDtypeStruct((bsz, s, CONV_A_WIDTH), BF16),
                   jax.ShapeDtypeStruct((bsz, s, GDN_WIDTH), BF16),
                   jax.ShapeDtypeStruct((bsz, s, GDN_WIDTH), BF16),
                   jax.ShapeDtypeStruct((bsz, s, GDN_WIDTH), BF16),
                   jax.ShapeDtypeStruct((bsz, s, GDN_WIDTH), BF16),
                   jax.ShapeDtypeStruct((bsz, s, LANES), F32)],
        scratch_shapes=[pltpu.VMEM((tm + HALO, CONV_A_WIDTH), F32),
                        pltpu.VMEM((tm + HALO, 3 * GDN_WIDTH), F32)],
        compiler_params=pltpu.CompilerParams(dimension_semantics=("arbitrary", "arbitrary"),
                                             vmem_limit_bytes=VMEM_LIMIT),
        name="even_in_proj",
    )(x, w, wba, ca, cb)


def _gdn_body(q_ref, k_ref, v_ref, zg_ref, ba_ref, arow_ref, acol_ref, gn_ref,
              o_ref, s_scr, *, tb, bb):
    nchunk = tb // CHUNK

    @pl.when(pl.program_id(1) == 0)
    def _():
        s_scr[...] = jnp.zeros(s_scr.shape, F32)

    ti = lax.broadcasted_iota(jnp.int32, (tb, tb), 0)
    tj = lax.broadcasted_iota(jnp.int32, (tb, tb), 1)
    same_chunk = (ti // CHUNK) == (tj // CHUNK)
    lower_blk = jnp.where(same_chunk & (tj <= ti), 1.0, 0.0).astype(F32)
    upper_blk = jnp.where(same_chunk & (ti <= tj), 1.0, 0.0).astype(F32)
    beta_cols, gc_cols, gc_rows = [], [], []
    for bi in range(bb):
        ba = ba_ref[bi]
        beta_cols.append(_sigmoid(ba))
        g_cols = -jnp.exp(arow_ref[0:1, :]) * _softplus(ba + arow_ref[1:2, :])
        ba_rows = jnp.transpose(ba)[0:2 * GDN_HEADS, :]
        g_rows = -jnp.exp(acol_ref[:, 0:1]) * _softplus(ba_rows + acol_ref[:, 1:2])
        gc_cols.append(_mm_f32(lower_blk, g_cols))
        gc_rows.append(_mm_f32(g_rows, upper_blk))

    pair_w = 2 * LANES
    row = lax.broadcasted_iota(jnp.int32, (CHUNK, 2 * CHUNK), 0)
    lane = lax.broadcasted_iota(jnp.int32, (CHUNK, 2 * CHUNK), 1)
    col = lane & (CHUNK - 1)
    first = lane < CHUNK
    eye = jnp.where(row == col, 1.0, 0.0).astype(F32)
    level_masks = [_pair_mask(row, col, lb) for lb in range(int(math.log2(CHUNK)))]
    lane1 = lax.broadcasted_iota(jnp.int32, (1, LANES), 1)
    scale = HEAD_DIM ** -0.5
    gn = gn_ref[...]

    def blockdiag_c(a, keep=None):
        top = first if keep is None else keep & first
        bot = ~first if keep is None else keep & ~first
        return jnp.concatenate([jnp.where(top, a, 0.0), jnp.where(bot, a, 0.0)], axis=0).astype(BF16)

    def per_head(c0, c1):
        return jnp.concatenate([jnp.broadcast_to(c0, (CHUNK, LANES)), jnp.broadcast_to(c1, (CHUNK, LANES))],
                               axis=1)

    probs = [(bi, hp, ci) for bi in range(bb) for hp in range(GDN_HEADS // 2) for ci in range(nchunk)]
    pr = []
    for bi, hp, ci in probs:
        h0, h1 = 2 * hp, 2 * hp + 1
        rs = slice(CHUNK * ci, CHUNK * (ci + 1))
        ws = slice(pair_w * hp, pair_w * (hp + 1))
        d = dict(bi=bi, hp=hp, rs=rs)
        q = q_ref[bi, rs, ws].astype(F32) * scale
        d["k_b"] = k_ref[bi, rs, ws]
        k = d["k_b"].astype(F32)
        v = v_ref[bi, rs, ws].astype(F32)
        gcc = [gc_cols[bi][rs, GDN_HEADS + h:GDN_HEADS + h + 1] for h in (h0, h1)]
        g_last = [g[CHUNK - 1:CHUNK, :] for g in gcc]
        tile = slice(LANES * (ci // 2), LANES * (ci // 2 + 1))
        r0 = gc_rows[bi][GDN_HEADS + h0:GDN_HEADS + h0 + 1, tile]
        r1 = gc_rows[bi][GDN_HEADS + h1:GDN_HEADS + h1 + 1, tile]
        if ci % 2:
            r0 = pltpu.roll(r0, CHUNK, axis=1)
        else:
            r1 = pltpu.roll(r1, CHUNK, axis=1)
        gcr = jnp.where(lane1 < CHUNK, r0, r1)
        gcc2 = jnp.where(first, gcc[0], gcc[1])
        d["decay"] = jnp.exp(jnp.where(row >= col, gcc2 - gcr, NEG_BIG))
        beta = per_head(beta_cols[bi][rs, h0:h0 + 1], beta_cols[bi][rs, h1:h1 + 1])
        e_gc = per_head(jnp.exp(gcc[0]), jnp.exp(gcc[1]))
        kb = k * beta
        d["kbq"] = jnp.concatenate([kb, q], axis=0).astype(BF16)
        vb, kbe = (v * beta).astype(BF16), (kb * e_gc).astype(BF16)
        z = jnp.zeros((CHUNK, pair_w), BF16)
        d["rhs"] = jnp.concatenate(
            [jnp.concatenate([vb[:, :LANES], kbe[:, :LANES], z], axis=1),
             jnp.concatenate([z, vb[:, LANES:], kbe[:, LANES:]], axis=1)], axis=0)
        d["q_dec"] = (q * e_gc).astype(BF16)
        d["k_dec"] = (k * per_head(jnp.exp(g_last[0] - gcc[0]), jnp.exp(g_last[1] - gcc[1]))).astype(BF16)
        d["s_decay"] = [jnp.exp(g) for g in g_last]
        pr.append(d)

    for d in pr:
        both = _mm_nt(d["kbq"], _blockdiag2(d["k_b"]))
        d["low"] = jnp.where(row > col, both[:CHUNK] * d["decay"], 0.0)
        d["attn"] = (both[CHUNK:] * d["decay"]).astype(BF16)
        d["tinv"] = eye - jnp.where(level_masks[0], d["low"], 0.0)
    for lb in range(1, len(level_masks)):
        for d in pr:
            d["p"] = _mm(d["tinv"], blockdiag_c(d["low"], level_masks[lb]))
        for d in pr:
            d["tinv"] = d["tinv"] - _mm(d["p"], blockdiag_c(d["tinv"]))
    for d in pr:
        d["uw"] = _mm(d["tinv"], d["rhs"])

    states = [s_scr[i] for i in range(bb * GDN_HEADS)]
    for ci in range(nchunk):
        cur = [d for d, (_, _, c) in zip(pr, probs) if c == ci]
        for d in cur:
            d["wq_s"] = []
            for i in range(2):
                ls = slice(LANES * i, LANES * (i + 1))
                w = d["uw"][:, pair_w * i + LANES:pair_w * (i + 1)].astype(BF16)
                d["wq_s"].append(_mm(jnp.concatenate([w, d["q_dec"][:, ls]], axis=0),
                                     states[d["bi"] * GDN_HEADS + 2 * d["hp"] + i]))
        for d in cur:
            v_new = jnp.concatenate([d["uw"][:, pair_w * i:pair_w * i + LANES] - d["wq_s"][i][:CHUNK]
                                     for i in range(2)], axis=1).astype(BF16)
            o_pair = (jnp.concatenate([d["wq_s"][i][CHUNK:] for i in range(2)], axis=1)
                      + _mm(d["attn"], _blockdiag2(v_new)))
            for i in range(2):
                h = 2 * d["hp"] + i
                si = d["bi"] * GDN_HEADS + h
                ls = slice(LANES * i, LANES * (i + 1))
                hs = slice(LANES * h, LANES * (h + 1))
                states[si] = states[si] * d["s_decay"][i] + _mm_tn(d["k_dec"][:, ls], v_new[:, ls])
                o = o_pair[:, ls]
                o = o * lax.rsqrt(jnp.mean(o * o, axis=-1, keepdims=True) + NORM_EPS) * gn
                o_ref[d["bi"], d["rs"], hs] = (o * zg_ref[d["bi"], d["rs"], hs].astype(F32)).astype(BF16)
    for i in range(bb * GDN_HEADS):
        s_scr[i] = states[i]


def _gdn_call(q, k, v, zg, ba, arow, acol, gn, *, tb, bb):
    bsz, s, _ = q.shape
    row = lambda w: pl.BlockSpec((bb, tb, w), lambda b, t: (b, t, 0))
    return pl.pallas_call(
        functools.partial(_gdn_body, tb=tb, bb=bb),
        grid=(bsz // bb, s // tb),
        in_specs=[row(GDN_WIDTH), row(GDN_WIDTH), row(GDN_WIDTH), row(GDN_WIDTH), row(LANES),
                  _const_spec(arow.shape), _const_spec(acol.shape), _const_spec(gn.shape)],
        out_specs=row(GDN_WIDTH),
        out_shape=jax.ShapeDtypeStruct((bsz, s, GDN_WIDTH), BF16),
        scratch_shapes=[pltpu.VMEM((bb * GDN_HEADS, HEAD_DIM, HEAD_DIM), F32)],
        compiler_params=pltpu.CompilerParams(dimension_semantics=("arbitrary", "arbitrary"),
                                             vmem_limit_bytes=VMEM_LIMIT),
        name="gated_delta_rule",
    )(q, k, v, zg, ba, arow, acol, gn)


def _out_body(*refs, n_y):
    y_refs, (x_ref, p_ref) = refs[:n_y], refs[n_y:n_y + 2]
    wo_refs = refs[n_y + 2:2 * n_y + 2]
    wg_ref, wp_ref, lg_ref, lb_ref, o_ref = refs[2 * n_y + 2:]
    tm = x_ref.shape[0]
    subs = [slice(r, r + OUT_SUB_ROWS) for r in range(0, tm, OUT_SUB_ROWS)]
    s_parts, embs, xns, gates = [], [], [], []
    for rs in subs:
        s = jnp.dot(y_refs[0][rs, :], wo_refs[0][...], preferred_element_type=F32)
        for y_ref, wo_ref in zip(y_refs[1:], wo_refs[1:]):
            s = s + jnp.dot(y_ref[rs, :], wo_ref[...], preferred_element_type=F32)
        s_parts.append(s)
    for rs in subs:
        embs.append(jnp.dot(p_ref[rs, :].astype(BF16), wp_ref[...], preferred_element_type=F32))
    for rs, s in zip(subs, s_parts):
        t = DEEPNORM_ALPHA * x_ref[rs, :] + s
        mu = jnp.mean(t, axis=-1, keepdims=True)
        tc = t - mu
        var = jnp.mean(tc * tc, axis=-1, keepdims=True)
        xn = tc * lax.rsqrt(var + NORM_EPS) * lg_ref[...] + lb_ref[...]
        xns.append(xn)
        gates.append(jnp.dot(xn.astype(BF16), wg_ref[...], preferred_element_type=F32))
    for rs, xn, emb, g in zip(subs, xns, embs, gates):
        o_ref[rs, :] = xn + emb * _sigmoid(g)


def _out_call(ys, x, p, layer, wos, wg, wp, lg, lb, *, tm):
    n, d = x.shape
    row = lambda w: pl.BlockSpec((tm, w), lambda i: (i, 0))
    p_spec = pl.BlockSpec((None, tm, p.shape[2]), lambda i: (layer, i, 0))
    return pl.pallas_call(
        functools.partial(_out_body, n_y=len(ys)),
        grid=(n // tm,),
        in_specs=([row(y.shape[1]) for y in ys] + [row(d), p_spec]
                  + [_const_spec(w.shape) for w in wos]
                  + [_const_spec(wg.shape), _const_spec(wp.shape), _const_spec(lg.shape), _const_spec(lb.shape)]),
        out_specs=row(d),
        out_shape=jax.ShapeDtypeStruct((n, d), F32),
        compiler_params=pltpu.CompilerParams(dimension_semantics=("arbitrary",),
                                             vmem_limit_bytes=VMEM_LIMIT),
        name="out_proj_norm_gate",
    )(*ys, x, p, *wos, wg, wp, lg, lb)


def _odd_in_body(x_ref, w_ref, lbr_ref, q_ref, f_ref, v_ref, zg_ref):
    xt = x_ref[...].astype(BF16)
    lb_raw = lbr_ref[...]
    e = jnp.exp(lb_raw - jnp.max(lb_raw, axis=0, keepdims=True))
    sm = e / jnp.sum(e, axis=0, keepdims=True)
    lower = (sm[0:1, :] + sm[1:2, :]) - sm[0:1, :]
    w = HGRN_WIDTH
    q_ref[...] = _silu(jnp.dot(xt, w_ref[:, 0:w], preferred_element_type=F32)).astype(BF16)
    f_raw = jnp.dot(xt, w_ref[:, w:2 * w], preferred_element_type=F32)
    f_ref[...] = lower + (1.0 - lower) * _sigmoid(f_raw)
    v_ref[...] = jnp.dot(xt, w_ref[:, 2 * w:3 * w], preferred_element_type=F32).astype(BF16)
    zg_ref[...] = _silu(jnp.dot(xt, w_ref[:, 3 * w:4 * w], preferred_element_type=F32)).astype(BF16)


def _odd_in_call(x, w, lbr, *, tm):
    n, d = x.shape
    row = lambda wd: pl.BlockSpec((tm, wd), lambda i: (i, 0))
    return pl.pallas_call(
        _odd_in_body,
        grid=(n // tm,),
        in_specs=[row(d), _const_spec(w.shape), _const_spec(lbr.shape)],
        out_specs=[row(HGRN_WIDTH)] * 4,
        out_shape=[jax.ShapeDtypeStruct((n, HGRN_WIDTH), BF16),
                   jax.ShapeDtypeStruct((n, HGRN_WIDTH), F32),
                   jax.ShapeDtypeStruct((n, HGRN_WIDTH), BF16),
                   jax.ShapeDtypeStruct((n, HGRN_WIDTH), BF16)],
        compiler_params=pltpu.CompilerParams(dimension_semantics=("arbitrary",),
                                             vmem_limit_bytes=VMEM_LIMIT),
        name="odd_in_proj",
    )(x, w, lbr)


GROUP = SUBLANES
NGROUP = CHUNK // GROUP


def _group_row_masks(w):
    r = lax.broadcasted_iota(jnp.int32, (NGROUP, GROUP, w), 1)
    return dict(odd=(r & 1) == 1, hi2=(r & 2) != 0, hi4=(r & 4) != 0)


def _decay_products(f, m):
    w = f.shape[-1]
    f3 = f.reshape(NGROUP, GROUP, w)
    odd, hi2, hi4 = m["odd"], m["hi2"], m["hi4"]
    on_rows = lambda x, i: jnp.broadcast_to(x[:, i:i + 1, :], x.shape)
    p2 = f3 * jnp.where(odd, pltpu.roll(f3, 1, axis=1), 1.0)
    p4 = p2 * jnp.where(hi2, jnp.where(hi4, on_rows(p2, 5), on_rows(p2, 1)), 1.0)
    p8 = p4 * jnp.where(hi4, on_rows(p4, 3), 1.0)
    s2 = jnp.where(odd, 1.0, pltpu.roll(f3, GROUP - 1, axis=1))
    s4 = s2 * jnp.where(hi2, 1.0, jnp.where(hi4, on_rows(p2, 7), on_rows(p2, 3)))
    s8 = s4 * jnp.where(hi4, 1.0, on_rows(p4, 7))
    fac = {1: f3, 2: jnp.where(hi2, p2, s2), 4: jnp.where(hi4, p4, s4)}
    return {b: e.reshape(CHUNK, w) for b, e in fac.items()}, p8, s8


def _small_operand(q, k, e, log2_b, upper_rows):
    if log2_b == 0:
        return jnp.where(upper_rows, q * e, k).astype(BF16)
    return (jnp.where(upper_rows, q, k) * e).astype(BF16)


def _group_operands(q, k, p8, s8):
    w = q.shape[-1]
    qp = q.reshape(NGROUP, GROUP, w) * p8
    ks = k.reshape(NGROUP, GROUP, w) * s8
    tot = [p8[g, GROUP - 1:GROUP, :] for g in range(NGROUP)]

    def prefix_scaled(lo, hi):
        out, run = [qp[lo]], None
        for g in range(lo + 1, hi):
            run = tot[g - 1] if run is None else run * tot[g - 1]
            out.append(qp[g] * run)
        return out, (tot[hi - 1] if run is None else run * tot[hi - 1])

    def suffix_scaled(lo, hi):
        out, run = [ks[hi - 1]], None
        for g in range(hi - 2, lo - 1, -1):
            run = tot[g + 1] if run is None else run * tot[g + 1]
            out.append(ks[g] * run)
        return out[::-1]

    ops, b = [], GROUP
    while b < CHUNK:
        gpb = b // GROUP
        pieces = []
        for blk in range(NGROUP // gpb):
            lo, hi = blk * gpb, (blk + 1) * gpb
            pieces += prefix_scaled(lo, hi)[0] if blk % 2 else suffix_scaled(lo, hi)
        ops.append(jnp.concatenate(pieces, axis=0).astype(BF16))
        b *= 2
    q_inc, total = prefix_scaled(0, NGROUP)
    k_exc = suffix_scaled(0, NGROUP)
    return ops, jnp.concatenate(q_inc, axis=0).astype(BF16), jnp.concatenate(k_exc, axis=0).astype(BF16), total


def _hgrn_body(q_ref, f_ref, v_ref, zg_ref, gn_ref, o_ref, s_scr, *, tb, hb):
    nchunk = tb // CHUNK
    nlev = int(math.log2(CHUNK))
    pair_w = 2 * LANES

    @pl.when(pl.program_id(2) == 0)
    def _():
        s_scr[...] = jnp.zeros(s_scr.shape, F32)

    row = lax.broadcasted_iota(jnp.int32, (CHUNK, 2 * CHUNK), 0)
    col = lax.broadcasted_iota(jnp.int32, (CHUNK, 2 * CHUNK), 1) & (CHUNK - 1)
    level_masks = [_pair_mask(row, col, lb) for lb in range(nlev)]
    trow = lax.broadcasted_iota(jnp.int32, (CHUNK, pair_w), 0)
    upper_rows = [(jnp.right_shift(trow, lb) & 1) == 1 for lb in range(int(math.log2(GROUP)))]
    group_masks = _group_row_masks(pair_w)
    gn = gn_ref[...]

    probs = [(hp, ci) for hp in range(hb // 2) for ci in range(nchunk)]
    states = {}

    def stage_a(hp, ci):
        d = dict(hp=hp, ws=slice(pair_w * hp, pair_w * (hp + 1)), rs=slice(CHUNK * ci, CHUNK * (ci + 1)))
        f = f_ref[0, d["rs"], d["ws"]]
        q = q_ref[0, d["rs"], d["ws"]].astype(F32)
        k = 1.0 - f
        d["v_b"] = v_ref[0, d["rs"], d["ws"]]
        fac, p8, s8 = _decay_products(f, group_masks)
        group_ops, d["q_inc"], k_exc, d["s_decay"] = _group_operands(q, k, p8, s8)
        ops = [_small_operand(q, k, fac[1 << lb], lb, upper_rows[lb]) for lb in range(len(upper_rows))]
        q_b, k_b = q.astype(BF16), k.astype(BF16)
        d["scores"] = [_mm_nt(q_b, _blockdiag2(k_b))]
        d["scores"] += [_mm_nt(p, _blockdiag2(p)) for p in ops + group_ops]
        d["kv"] = [_mm_tn(d["v_b"][:, LANES * i:LANES * (i + 1)], k_exc[:, LANES * i:LANES * (i + 1)])
                   for i in range(2)]
        return d

    def stage_b(d):
        attn = jnp.where(row == col, d["scores"][0], 0.0)
        for lb in range(nlev):
            attn = jnp.where(level_masks[lb], d["scores"][lb + 1], attn)
        d["o_intra"] = _mm(attn, _blockdiag2(d["v_b"]))

    def stage_c(d):
        for i in range(2):
            h = 2 * d["hp"] + i
            if h not in states:
                states[h] = s_scr[h]
            ls = slice(LANES * i, LANES * (i + 1))
            o = d["o_intra"][:, ls] + _mm_nt(d["q_inc"][:, ls], states[h])
            states[h] = states[h] * d["s_decay"][:, ls] + d["kv"][i]
            o = o * lax.rsqrt(jnp.mean(o * o, axis=-1, keepdims=True) + NORM_EPS) * gn
            os = slice(LANES * h, LANES * (h + 1))
            o_ref[0, d["rs"], os] = (o * zg_ref[0, d["rs"], os].astype(F32)).astype(BF16)

    done_a = []
    for n in range(len(probs) + 2 * PIPE_LAG):
        if n < len(probs):
            done_a.append(stage_a(*probs[n]))
        if 0 <= n - PIPE_LAG < len(probs):
            stage_b(done_a[n - PIPE_LAG])
        if 0 <= n - 2 * PIPE_LAG < len(probs):
            stage_c(done_a[n - 2 * PIPE_LAG])
    for h in range(hb):
        s_scr[h] = states[h]


def _hgrn_call(q, f, v, zg, gn, *, tb, hb):
    bsz, s, w = q.shape
    blk = lambda: pl.BlockSpec((1, tb, hb * LANES), lambda b, g, t: (b, t, g))
    return pl.pallas_call(
        functools.partial(_hgrn_body, tb=tb, hb=hb),
        grid=(bsz, w // (hb * LANES), s // tb),
        in_specs=[blk(), blk(), blk(), blk(), _const_spec(gn.shape)],
        out_specs=blk(),
        out_shape=jax.ShapeDtypeStruct((bsz, s, w), BF16),
        scratch_shapes=[pltpu.VMEM((hb, HEAD_DIM, HEAD_DIM), F32)],
        compiler_params=pltpu.CompilerParams(dimension_semantics=("arbitrary", "arbitrary", "arbitrary"),
                                             vmem_limit_bytes=VMEM_LIMIT),
        name="hgrn2_recurrence",
    )(q, f, v, zg, gn)


def kernel(x, p, w_in_even, conv_a_w, conv_b_w, a_log, dt_bias, gdn_norm_g, w_out_even, w_in_odd,
           lower_bounds, hgrn_norm_g, w_out_odd, ln_g, ln_b, w_pl, w_pl_gate):
    bsz, s, d = x.shape
    n = bsz * s
    assert d == D_MODEL and p.shape == (DEPTH, bsz, s, PL_DIM)
    assert s % max(IN_ROWS, GDN_ROWS, HGRN_ROWS) == 0 and n % OUT_ROWS == 0 and bsz % GDN_BATCH_ROWS == 0
    zb_end = 4 * CONV_A_WIDTH + 4 * GDN_WIDTH

    w_main = w_in_even[0, :, :zb_end].astype(BF16)
    wba = jnp.pad(w_in_even[0, :, zb_end:].astype(BF16), ((0, 0), (0, LANES - 2 * GDN_HEADS)))
    ya, q, k, v, zg, ba = _even_in_call(x, w_main, wba, conv_a_w[0], conv_b_w[0], tm=IN_ROWS)

    pad_lane = lambda a: jnp.pad(a, (GDN_HEADS, LANES - 2 * GDN_HEADS))
    arow = jnp.stack([pad_lane(a_log[0]), pad_lane(dt_bias[0])], axis=0)
    acol = jnp.stack([jnp.pad(a_log[0], (GDN_HEADS, 0)), jnp.pad(dt_bias[0], (GDN_HEADS, 0))], axis=1)
    og = _gdn_call(q, k, v, zg, ba, arow, acol, gdn_norm_g[0][None, :], tb=GDN_ROWS, bb=GDN_BATCH_ROWS)

    w_out = w_out_even[0].astype(BF16)
    p_rows = p.reshape(DEPTH, n, PL_DIM)
    x1 = _out_call([ya.reshape(n, -1), og.reshape(n, -1)], x.reshape(n, d), p_rows, 0,
                   [w_out[:CONV_A_WIDTH], w_out[CONV_A_WIDTH:]], w_pl_gate[0].astype(BF16),
                   w_pl[0].astype(BF16), ln_g[0][None, :], ln_b[0][None, :], tm=OUT_ROWS)

    hq, hf, hv, hzg = _odd_in_call(x1, w_in_odd[0].astype(BF16), lower_bounds, tm=IN_ROWS)
    shp = (bsz, s, HGRN_WIDTH)
    ho = _hgrn_call(hq.reshape(shp), hf.reshape(shp), hv.reshape(shp), hzg.reshape(shp),
                    hgrn_norm_g[0][None, :], tb=HGRN_ROWS, hb=HGRN_STEP_HEADS)
    out = _out_call([ho.reshape(n, -1)], x1, p_rows, 1, [w_out_odd[0].astype(BF16)],
                    w_pl_gate[1].astype(BF16), w_pl[1].astype(BF16), ln_g[1][None, :], ln_b[1][None, :],
                    tm=OUT_ROWS)
    return out.reshape(bsz, s, d)
```

```python
import functools
import math

import jax
import jax.numpy as jnp
from jax import lax
from jax.experimental import pallas as pl
from jax.experimental.pallas import tpu as pltpu

F32 = jnp.float32
BF16 = jnp.bfloat16

D_MODEL = 1024
DEPTH = 2
PL_DIM = 256
CONV_A_WIDTH = 1024
CONV_A_KERNEL = 3
GDN_HEADS = 8
HEAD_DIM = 128
GDN_WIDTH = GDN_HEADS * HEAD_DIM
GDN_CONV_KERNEL = 4
HGRN_WIDTH = 2 * D_MODEL
HGRN_HEADS = HGRN_WIDTH // HEAD_DIM
DEEPNORM_ALPHA = (2.0 * DEPTH) ** 0.25
NORM_EPS = 1e-5
L2_EPS = 1e-6

LANES = 128
SUBLANES = 8
MXU_COLS = 256
OUT_SUB_ROWS = 128
PIPE_LAG = 2
CHUNK = 64
HALO = SUBLANES
NEG_BIG = -1e30

VMEM_CAPACITY = 64 * 1024 * 1024
VMEM_LIMIT = VMEM_CAPACITY * 7 // 8

IN_ROWS = 256
OUT_ROWS = 1024
GDN_ROWS, GDN_BATCH_ROWS = 256, 2
HGRN_ROWS, HGRN_STEP_HEADS = 512, 8


def _sigmoid(x):
    return 1.0 / (1.0 + jnp.exp(-x))


def _silu(x):
    return x * _sigmoid(x)


def _softplus(x):
    return jnp.maximum(x, 0.0) + jnp.log1p(jnp.exp(-jnp.abs(x)))


def _mm(a, b):
    return jnp.dot(a.astype(BF16), b.astype(BF16), preferred_element_type=F32)


def _mm_nt(a, b):
    return lax.dot_general(a.astype(BF16), b.astype(BF16), (((1,), (1,)), ((), ())),
                           preferred_element_type=F32)


def _mm_tn(a, b):
    return lax.dot_general(a.astype(BF16), b.astype(BF16), (((0,), (0,)), ((), ())),
                           preferred_element_type=F32)


def _mm_f32(a, b):
    return jnp.dot(a, b, preferred_element_type=F32, precision=lax.Precision.HIGHEST)


def _const_spec(shape):
    nd = len(shape)
    return pl.BlockSpec(shape, lambda *_: (0,) * nd, pipeline_mode=pl.Buffered(1))


def _pair_mask(row, col, log2_b):
    return (jnp.right_shift(jnp.bitwise_xor(row, col), log2_b) == 1) & (row > col)


def _blockdiag2(a):
    z = jnp.zeros((a.shape[0], LANES), a.dtype)
    return jnp.concatenate([jnp.concatenate([a[:, :LANES], z], axis=1),
                            jnp.concatenate([z, a[:, LANES:]], axis=1)], axis=0)


def _even_in_body(x_ref, w_ref, wba_ref, ca_ref, cb_ref,
                  ya_ref, q_ref, k_ref, v_ref, zg_ref, ba_ref, ua_scr, qkv_scr, *, tm):
    @pl.when(pl.program_id(1) == 0)
    def _():
        ua_scr[0:HALO, :] = jnp.zeros((HALO, CONV_A_WIDTH), F32)
        qkv_scr[0:HALO, :] = jnp.zeros((HALO, 3 * GDN_WIDTH), F32)

    xt = x_ref[0].astype(BF16)
    proj = lambda lo: jnp.dot(xt, w_ref[:, lo:lo + MXU_COLS], preferred_element_type=F32)
    qkv_lo, zb_lo = 4 * CONV_A_WIDTH, 4 * CONV_A_WIDTH + 3 * GDN_WIDTH

    def conv_taps(scr, w_ref, cur, sl):
        taps = w_ref.shape[0]
        scr[HALO:HALO + tm, sl] = cur
        ext = scr[0:HALO + tm, sl]
        acc = w_ref[taps - 1:taps, sl] * cur
        for j in range(1, taps):
            acc = acc + w_ref[taps - 1 - j:taps - j, sl] * pltpu.roll(ext, j, axis=0)[HALO:]
        scr[0:HALO, sl] = scr[tm:tm + HALO, sl]
        return acc

    tasks = []

    def mixer_a(g):
        sl = slice(MXU_COLS * g, MXU_COLS * (g + 1))
        def mm():
            return [proj(part * CONV_A_WIDTH + MXU_COLS * g) for part in range(4)]
        def ew(r):
            h, c, b, z = r
            ya_ref[0, :, sl] = (b * conv_taps(ua_scr, ca_ref, c * h, sl) * _silu(z)).astype(BF16)
        return mm, ew

    def mixer_b_in(g):
        sl = slice(MXU_COLS * g, MXU_COLS * (g + 1))
        def mm():
            return proj(qkv_lo + MXU_COLS * g)
        def ew(r):
            a = _silu(conv_taps(qkv_scr, cb_ref, r, sl))
            for i in range(MXU_COLS // LANES):
                part, hj = divmod(g * (MXU_COLS // LANES) + i, GDN_HEADS)
                hs = slice(LANES * hj, LANES * (hj + 1))
                ah = a[:, LANES * i:LANES * (i + 1)]
                if part < 2:
                    ah = ah * lax.rsqrt(jnp.sum(ah * ah, axis=-1, keepdims=True) + L2_EPS)
                (q_ref, k_ref, v_ref)[part][0, :, hs] = ah.astype(BF16)
        return mm, ew

    def gate_b(g):
        sl = slice(MXU_COLS * g, MXU_COLS * (g + 1))
        def mm():
            return proj(zb_lo + MXU_COLS * g)
        def ew(r):
            zg_ref[0, :, sl] = _silu(r).astype(BF16)
        return mm, ew

    n_a, n_b = CONV_A_WIDTH // MXU_COLS, 3 * GDN_WIDTH // MXU_COLS
    for i in range(n_a):
        tasks.append(mixer_a(i))
        tasks += [mixer_b_in(g) for g in range(i * n_b // n_a, (i + 1) * n_b // n_a)]
        tasks.append(gate_b(i))
    for mm, ew in tasks:
        ew(mm())
    ba_ref[0] = jnp.dot(xt, wba_ref[...], preferred_element_type=F32)


def _even_in_call(x, w, wba, ca, cb, *, tm):
    bsz, s, _ = x.shape
    row = lambda w: pl.BlockSpec((1, tm, w), lambda b, t: (b, t, 0))
    return pl.pallas_call(
        functools.partial(_even_in_body, tm=tm),
        grid=(bsz, s // tm),
        in_specs=[row(D_MODEL), _const_spec(w.shape), _const_spec(wba.shape), _const_spec(ca.shape),
                  _const_spec(cb.shape)],
        out_specs=[row(CONV_A_WIDTH), row(GDN_WIDTH), row(GDN_WIDTH), row(GDN_WIDTH), row(GDN_WIDTH),
                   row(LANES)],
        out_shape=[jax.ShapeDtypeStruct((bsz, s, CONV_A_WIDTH), BF16),
                   jax.ShapeDtypeStruct((bsz, s, GDN_WIDTH), BF16),
                   jax.ShapeDtypeStruct((bsz, s, GDN_WIDTH), BF16),
                   jax.ShapeDtypeStruct((bsz, s, GDN_WIDTH), BF16),
                   jax.ShapeDtypeStruct((bsz, s, GDN_WIDTH), BF16),
                   jax.ShapeDtypeStruct((bsz, s, LANES), F32)],
        scratch_shapes=[pltpu.VMEM((tm + HALO, CONV_A_WIDTH), F32),
                        pltpu.VMEM((tm + HALO, 3 * GDN_WIDTH), F32)],
        compiler_params=pltpu.CompilerParams(dimension_semantics=("arbitrary", "arbitrary"),
                                             vmem_limit_bytes=VMEM_LIMIT),
        name="even_in_proj",
    )(x, w, wba, ca, cb)


def _gdn_body(q_ref, k_ref, v_ref, zg_ref, ba_ref, arow_ref, acol_ref, gn_ref,
              o_ref, s_scr, *, tb, bb):
    nchunk = tb // CHUNK

    @pl.when(pl.program_id(1) == 0)
    def _():
        s_scr[...] = jnp.zeros(s_scr.shape, F32)

    ti = lax.broadcasted_iota(jnp.int32, (tb, tb), 0)
    tj = lax.broadcasted_iota(jnp.int32, (tb, tb), 1)
    same_chunk = (ti // CHUNK) == (tj // CHUNK)
    lower_blk = jnp.where(same_chunk & (tj <= ti), 1.0, 0.0).astype(F32)
    upper_blk = jnp.where(same_chunk & (ti <= tj), 1.0, 0.0).astype(F32)
    beta_cols, gc_cols, gc_rows = [], [], []
    for bi in range(bb):
        ba = ba_ref[bi]
        beta_cols.append(_sigmoid(ba))
        g_cols = -jnp.exp(arow_ref[0:1, :]) * _softplus(ba + arow_ref[1:2, :])
        ba_rows = jnp.transpose(ba)[0:2 * GDN_HEADS, :]
        g_rows = -jnp.exp(acol_ref[:, 0:1]) * _softplus(ba_rows + acol_ref[:, 1:2])
        gc_cols.append(_mm_f32(lower_blk, g_cols))
        gc_rows.append(_mm_f32(g_rows, upper_blk))

    pair_w = 2 * LANES
    row = lax.broadcasted_iota(jnp.int32, (CHUNK, 2 * CHUNK), 0)
    lane = lax.broadcasted_iota(jnp.int32, (CHUNK, 2 * CHUNK), 1)
    col = lane & (CHUNK - 1)
    first = lane < CHUNK
    eye = jnp.where(row == col, 1.0, 0.0).astype(F32)
    level_masks = [_pair_mask(row, col, lb) for lb in range(int(math.log2(CHUNK)))]
    lane1 = lax.broadcasted_iota(jnp.int32, (1, LANES), 1)
    scale = HEAD_DIM ** -0.5
    gn = gn_ref[...]

    def blockdiag_c(a, keep=None):
        top = first if keep is None else keep & first
        bot = ~first if keep is None else keep & ~first
        return jnp.concatenate([jnp.where(top, a, 0.0), jnp.where(bot, a, 0.0)], axis=0).astype(BF16)

    def per_head(c0, c1):
        return jnp.concatenate([jnp.broadcast_to(c0, (CHUNK, LANES)), jnp.broadcast_to(c1, (CHUNK, LANES))],
                               axis=1)

    probs = [(bi, hp, ci) for bi in range(bb) for hp in range(GDN_HEADS // 2) for ci in range(nchunk)]
    pr = []
    for bi, hp, ci in probs:
        h0, h1 = 2 * hp, 2 * hp + 1
        rs = slice(CHUNK * ci, CHUNK * (ci + 1))
        ws = slice(pair_w * hp, pair_w * (hp + 1))
        d = dict(bi=bi, hp=hp, rs=rs)
        q = q_ref[bi, rs, ws].astype(F32) * scale
        d["k_b"] = k_ref[bi, rs, ws]
        k = d["k_b"].astype(F32)
        v = v_ref[bi, rs, ws].astype(F32)
        gcc = [gc_cols[bi][rs, GDN_HEADS + h:GDN_HEADS + h + 1] for h in (h0, h1)]
        g_last = [g[CHUNK - 1:CHUNK, :] for g in gcc]
        tile = slice(LANES * (ci // 2), LANES * (ci // 2 + 1))
        r0 = gc_rows[bi][GDN_HEADS + h0:GDN_HEADS + h0 + 1, tile]
        r1 = gc_rows[bi][GDN_HEADS + h1:GDN_HEADS + h1 + 1, tile]
        if ci % 2:
            r0 = pltpu.roll(r0, CHUNK, axis=1)
        else:
            r1 = pltpu.roll(r1, CHUNK, axis=1)
        gcr = jnp.where(lane1 < CHUNK, r0, r1)
        gcc2 = jnp.where(first, gcc[0], gcc[1])
        d["decay"] = jnp.exp(jnp.where(row >= col, gcc2 - gcr, NEG_BIG))
        beta = per_head(beta_cols[bi][rs, h0:h0 + 1], beta_cols[bi][rs, h1:h1 + 1])
        e_gc = per_head(jnp.exp(gcc[0]), jnp.exp(gcc[1]))
        kb = k * beta
        d["kbq"] = jnp.concatenate([kb, q], axis=0).astype(BF16)
        vb, kbe = (v * beta).astype(BF16), (kb * e_gc).astype(BF16)
        z = jnp.zeros((CHUNK, pair_w), BF16)
        d["rhs"] = jnp.concatenate(
            [jnp.concatenate([vb[:, :LANES], kbe[:, :LANES], z], axis=1),
             jnp.concatenate([z, vb[:, LANES:], kbe[:, LANES:]], axis=1)], axis=0)
        d["q_dec"] = (q * e_gc).astype(BF16)
        d["k_dec"] = (k * per_head(jnp.exp(g_last[0] - gcc[0]), jnp.exp(g_last[1] - gcc[1]))).astype(BF16)
        d["s_decay"] = [jnp.exp(g) for g in g_last]
        pr.append(d)

    for d in pr:
        both = _mm_nt(d["kbq"], _blockdiag2(d["k_b"]))
        d["low"] = jnp.where(row > col, both[:CHUNK] * d["decay"], 0.0)
        d["attn"] = (both[CHUNK:] * d["decay"]).astype(BF16)
        d["tinv"] = eye - jnp.where(level_masks[0], d["low"], 0.0)
    for lb in range(1, len(level_masks)):
        for d in pr:
            d["p"] = _mm(d["tinv"], blockdiag_c(d["low"], level_masks[lb]))
        for d in pr:
            d["tinv"] = d["tinv"] - _mm(d["p"], blockdiag_c(d["tinv"]))
    for d in pr:
        d["uw"] = _mm(d["tinv"], d["rhs"])

    states = [s_scr[i] for i in range(bb * GDN_HEADS)]
    for ci in range(nchunk):
        cur = [d for d, (_, _, c) in zip(pr, probs) if c == ci]
        for d in cur:
            d["wq_s"] = []
            for i in range(2):
                ls = slice(LANES * i, LANES * (i + 1))
                w = d["uw"][:, pair_w * i + LANES:pair_w * (i + 1)].astype(BF16)
                d["wq_s"].append(_mm(jnp.concatenate([w, d["q_dec"][:, ls]], axis=0),
                                     states[d["bi"] * GDN_HEADS + 2 * d["hp"] + i]))
        for d in cur:
            v_new = jnp.concatenate([d["uw"][:, pair_w * i:pair_w * i + LANES] - d["wq_s"][i][:CHUNK]
                                     for i in range(2)], axis=1).astype(BF16)
            o_pair = (jnp.concatenate([d["wq_s"][i][CHUNK:] for i in range(2)], axis=1)
                      + _mm(d["attn"], _blockdiag2(v_new)))
            for i in range(2):
                h = 2 * d["hp"] + i
                si = d["bi"] * GDN_HEADS + h
                ls = slice(LANES * i, LANES * (i + 1))
                hs = slice(LANES * h, LANES * (h + 1))
                states[si] = states[si] * d["s_decay"][i] + _mm_tn(d["k_dec"][:, ls], v_new[:, ls])
                o = o_pair[:, ls]
                o = o * lax.rsqrt(jnp.mean(o * o, axis=-1, keepdims=True) + NORM_EPS) * gn
                o_ref[d["bi"], d["rs"], hs] = (o * zg_ref[d["bi"], d["rs"], hs].astype(F32)).astype(BF16)
    for i in range(bb * GDN_HEADS):
        s_scr[i] = states[i]


def _gdn_call(q, k, v, zg, ba, arow, acol, gn, *, tb, bb):
    bsz, s, _ = q.shape
    row = lambda w: pl.BlockSpec((bb, tb, w), lambda b, t: (b, t, 0))
    return pl.pallas_call(
        functools.partial(_gdn_body, tb=tb, bb=bb),
        grid=(bsz // bb, s // tb),
        in_specs=[row(GDN_WIDTH), row(GDN_WIDTH), row(GDN_WIDTH), row(GDN_WIDTH), row(LANES),
                  _const_spec(arow.shape), _const_spec(acol.shape), _const_spec(gn.shape)],
        out_specs=row(GDN_WIDTH),
        out_shape=jax.ShapeDtypeStruct((bsz, s, GDN_WIDTH), BF16),
        scratch_shapes=[pltpu.VMEM((bb * GDN_HEADS, HEAD_DIM, HEAD_DIM), F32)],
        compiler_params=pltpu.CompilerParams(dimension_semantics=("arbitrary", "arbitrary"),
                                             vmem_limit_bytes=VMEM_LIMIT),
        name="gated_delta_rule",
    )(q, k, v, zg, ba, arow, acol, gn)


def _out_body(*refs, n_y):
    y_refs, (x_ref, p_ref) = refs[:n_y], refs[n_y:n_y + 2]
    wo_refs = refs[n_y + 2:2 * n_y + 2]
    wg_ref, wp_ref, lg_ref, lb_ref, o_ref = refs[2 * n_y + 2:]
    tm = x_ref.shape[0]
    subs = [slice(r, r + OUT_SUB_ROWS) for r in range(0, tm, OUT_SUB_ROWS)]
    s_parts, embs, xns, gates = [], [], [], []
    for rs in subs:
        s = jnp.dot(y_refs[0][rs, :], wo_refs[0][...], preferred_element_type=F32)
        for y_ref, wo_ref in zip(y_refs[1:], wo_refs[1:]):
            s = s + jnp.dot(y_ref[rs, :], wo_ref[...], preferred_element_type=F32)
        s_parts.append(s)
    for rs in subs:
        embs.append(jnp.dot(p_ref[rs, :].astype(BF16), wp_ref[...], preferred_element_type=F32))
    for rs, s in zip(subs, s_parts):
        t = DEEPNORM_ALPHA * x_ref[rs, :] + s
        mu = jnp.mean(t, axis=-1, keepdims=True)
        tc = t - mu
        var = jnp.mean(tc * tc, axis=-1, keepdims=True)
        xn = tc * lax.rsqrt(var + NORM_EPS) * lg_ref[...] + lb_ref[...]
        xns.append(xn)
        gates.append(jnp.dot(xn.astype(BF16), wg_ref[...], preferred_element_type=F32))
    for rs, xn, emb, g in zip(subs, xns, embs, gates):
        o_ref[rs, :] = xn + emb * _sigmoid(g)


def _out_call(ys, x, p, layer, wos, wg, wp, lg, lb, *, tm):
    n, d = x.shape
    row = lambda w: pl.BlockSpec((tm, w), lambda i: (i, 0))
    p_spec = pl.BlockSpec((None, tm, p.shape[2]), lambda i: (layer, i, 0))
    return pl.pallas_call(
        functools.partial(_out_body, n_y=len(ys)),
        grid=(n // tm,),
        in_specs=([row(y.shape[1]) for y in ys] + [row(d), p_spec]
                  + [_const_spec(w.shape) for w in wos]
                  + [_const_spec(wg.shape), _const_spec(wp.shape), _const_spec(lg.shape), _const_spec(lb.shape)]),
        out_specs=row(d),
        out_shape=jax.ShapeDtypeStruct((n, d), F32),
        compiler_params=pltpu.CompilerParams(dimension_semantics=("arbitrary",),
                                             vmem_limit_bytes=VMEM_LIMIT),
        name="out_proj_norm_gate",
    )(*ys, x, p, *wos, wg, wp, lg, lb)


def _odd_in_body(x_ref, w_ref, lbr_ref, q_ref, f_ref, v_ref, zg_ref):
    xt = x_ref[...].astype(BF16)
    lb_raw = lbr_ref[...]
    e = jnp.exp(lb_raw - jnp.max(lb_raw, axis=0, keepdims=True))
    sm = e / jnp.sum(e, axis=0, keepdims=True)
    lower = (sm[0:1, :] + sm[1:2, :]) - sm[0:1, :]
    w = HGRN_WIDTH
    q_ref[...] = _silu(jnp.dot(xt, w_ref[:, 0:w], preferred_element_type=F32)).astype(BF16)
    f_raw = jnp.dot(xt, w_ref[:, w:2 * w], preferred_element_type=F32)
    f_ref[...] = lower + (1.0 - lower) * _sigmoid(f_raw)
    v_ref[...] = jnp.dot(xt, w_ref[:, 2 * w:3 * w], preferred_element_type=F32).astype(BF16)
    zg_ref[...] = _silu(jnp.dot(xt, w_ref[:, 3 * w:4 * w], preferred_element_type=F32)).astype(BF16)


def _odd_in_call(x, w, lbr, *, tm):
    n, d = x.shape
    row = lambda wd: pl.BlockSpec((tm, wd), lambda i: (i, 0))
    return pl.pallas_call(
        _odd_in_body,
        grid=(n // tm,),
        in_specs=[row(d), _const_spec(w.shape), _const_spec(lbr.shape)],
        out_specs=[row(HGRN_WIDTH)] * 4,
        out_shape=[jax.ShapeDtypeStruct((n, HGRN_WIDTH), BF16),
                   jax.ShapeDtypeStruct((n, HGRN_WIDTH), F32),
                   jax.ShapeDtypeStruct((n, HGRN_WIDTH), BF16),
                   jax.ShapeDtypeStruct((n, HGRN_WIDTH), BF16)],
        compiler_params=pltpu.CompilerParams(dimension_semantics=("arbitrary",),
                                             vmem_limit_bytes=VMEM_LIMIT),
        name="odd_in_proj",
    )(x, w, lbr)


GROUP = SUBLANES
NGROUP = CHUNK // GROUP


def _group_row_masks(w):
    r = lax.broadcasted_iota(jnp.int32, (NGROUP, GROUP, w), 1)
    return dict(odd=(r & 1) == 1, hi2=(r & 2) != 0, hi4=(r & 4) != 0)


def _decay_products(f, m):
    w = f.shape[-1]
    f3 = f.reshape(NGROUP, GROUP, w)
    odd, hi2, hi4 = m["odd"], m["hi2"], m["hi4"]
    on_rows = lambda x, i: jnp.broadcast_to(x[:, i:i + 1, :], x.shape)
    p2 = f3 * jnp.where(odd, pltpu.roll(f3, 1, axis=1), 1.0)
    p4 = p2 * jnp.where(hi2, jnp.where(hi4, on_rows(p2, 5), on_rows(p2, 1)), 1.0)
    p8 = p4 * jnp.where(hi4, on_rows(p4, 3), 1.0)
    s2 = jnp.where(odd, 1.0, pltpu.roll(f3, GROUP - 1, axis=1))
    s4 = s2 * jnp.where(hi2, 1.0, jnp.where(hi4, on_rows(p2, 7), on_rows(p2, 3)))
    s8 = s4 * jnp.where(hi4, 1.0, on_rows(p4, 7))
    fac = {1: f3, 2: jnp.where(hi2, p2, s2), 4: jnp.where(hi4, p4, s4)}
    return {b: e.reshape(CHUNK, w) for b, e in fac.items()}, p8, s8


def _small_operand(q, k, e, log2_b, upper_rows):
    if log2_b == 0:
        return jnp.where(upper_rows, q * e, k).astype(BF16)
    return (jnp.where(upper_rows, q, k) * e).astype(BF16)


def _group_operands(q, k, p8, s8):
    w = q.shape[-1]
    qp = q.reshape(NGROUP, GROUP, w) * p8
    ks = k.reshape(NGROUP, GROUP, w) * s8
    tot = [p8[g, GROUP - 1:GROUP, :] for g in range(NGROUP)]

    def prefix_scaled(lo, hi):
        out, run = [qp[lo]], None
        for g in range(lo + 1, hi):
            run = tot[g - 1] if run is None else run * tot[g - 1]
            out.append(qp[g] * run)
        return out, (tot[hi - 1] if run is None else run * tot[hi - 1])

    def suffix_scaled(lo, hi):
        out, run = [ks[hi - 1]], None
        for g in range(hi - 2, lo - 1, -1):
            run = tot[g + 1] if run is None else run * tot[g + 1]
            out.append(ks[g] * run)
        return out[::-1]

    ops, b = [], GROUP
    while b < CHUNK:
        gpb = b // GROUP
        pieces = []
        for blk in range(NGROUP // gpb):
            lo, hi = blk * gpb, (blk + 1) * gpb
            pieces += prefix_scaled(lo, hi)[0] if blk % 2 else suffix_scaled(lo, hi)
        ops.append(jnp.concatenate(pieces, axis=0).astype(BF16))
        b *= 2
    q_inc, total = prefix_scaled(0, NGROUP)
    k_exc = suffix_scaled(0, NGROUP)
    return ops, jnp.concatenate(q_inc, axis=0).astype(BF16), jnp.concatenate(k_exc, axis=0).astype(BF16), total


def _hgrn_body(q_ref, f_ref, v_ref, zg_ref, gn_ref, o_ref, s_scr, *, tb, hb):
    nchunk = tb // CHUNK
    nlev = int(math.log2(CHUNK))
    pair_w = 2 * LANES

    @pl.when(pl.program_id(2) == 0)
    def _():
        s_scr[...] = jnp.zeros(s_scr.shape, F32)

    row = lax.broadcasted_iota(jnp.int32, (CHUNK, 2 * CHUNK), 0)
    col = lax.broadcasted_iota(jnp.int32, (CHUNK, 2 * CHUNK), 1) & (CHUNK - 1)
    level_masks = [_pair_mask(row, col, lb) for lb in range(nlev)]
    trow = lax.broadcasted_iota(jnp.int32, (CHUNK, pair_w), 0)
    upper_rows = [(jnp.right_shift(trow, lb) & 1) == 1 for lb in range(int(math.log2(GROUP)))]
    group_masks = _group_row_masks(pair_w)
    gn = gn_ref[...]

    probs = [(hp, ci) for hp in range(hb // 2) for ci in range(nchunk)]
    states = {}

    def stage_a(hp, ci):
        d = dict(hp=hp, ws=slice(pair_w * hp, pair_w * (hp + 1)), rs=slice(CHUNK * ci, CHUNK * (ci + 1)))
        f = f_ref[0, d["rs"], d["ws"]]
        q = q_ref[0, d["rs"], d["ws"]].astype(F32)
        k = 1.0 - f
        d["v_b"] = v_ref[0, d["rs"], d["ws"]]
        fac, p8, s8 = _decay_products(f, group_masks)
        group_ops, d["q_inc"], k_exc, d["s_decay"] = _group_operands(q, k, p8, s8)
        ops = [_small_operand(q, k, fac[1 << lb], lb, upper_rows[lb]) for lb in range(len(upper_rows))]
        q_b, k_b = q.astype(BF16), k.astype(BF16)
        d["scores"] = [_mm_nt(q_b, _blockdiag2(k_b))]
        d["scores"] += [_mm_nt(p, _blockdiag2(p)) for p in ops + group_ops]
        d["kv"] = [_mm_tn(d["v_b"][:, LANES * i:LANES * (i + 1)], k_exc[:, LANES * i:LANES * (i + 1)])
                   for i in range(2)]
        return d

    def stage_b(d):
        attn = jnp.where(row == col, d["scores"][0], 0.0)
        for lb in range(nlev):
            attn = jnp.where(level_masks[lb], d["scores"][lb + 1], attn)
        d["o_intra"] = _mm(attn, _blockdiag2(d["v_b"]))

    def stage_c(d):
        for i in range(2):
            h = 2 * d["hp"] + i
            if h not in states:
                states[h] = s_scr[h]
            ls = slice(LANES * i, LANES * (i + 1))
            o = d["o_intra"][:, ls] + _mm_nt(d["q_inc"][:, ls], states[h])
            states[h] = states[h] * d["s_decay"][:, ls] + d["kv"][i]
            o = o * lax.rsqrt(jnp.mean(o * o, axis=-1, keepdims=True) + NORM_EPS) * gn
            os = slice(LANES * h, LANES * (h + 1))
            o_ref[0, d["rs"], os] = (o * zg_ref[0, d["rs"], os].astype(F32)).astype(BF16)

    done_a = []
    for n in range(len(probs) + 2 * PIPE_LAG):
        if n < len(probs):
            done_a.append(stage_a(*probs[n]))
        if 0 <= n - PIPE_LAG < len(probs):
            stage_b(done_a[n - PIPE_LAG])
        if 0 <= n - 2 * PIPE_LAG < len(probs):
            stage_c(done_a[n - 2 * PIPE_LAG])
    for h in range(hb):
        s_scr[h] = states[h]


def _hgrn_call(q, f, v, zg, gn, *, tb, hb):
    bsz, s, w = q.shape
    blk = lambda: pl.BlockSpec((1, tb, hb * LANES), lambda b, g, t: (b, t, g))
    return pl.pallas_call(
        functools.partial(_hgrn_body, tb=tb, hb=hb),
        grid=(bsz, w // (hb * LANES), s // tb),
        in_specs=[blk(), blk(), blk(), blk(), _const_spec(gn.shape)],
        out_specs=blk(),
        out_shape=jax.ShapeDtypeStruct((bsz, s, w), BF16),
        scratch_shapes=[pltpu.VMEM((hb, HEAD_DIM, HEAD_DIM), F32)],
        compiler_params=pltpu.CompilerParams(dimension_semantics=("arbitrary", "arbitrary", "arbitrary"),
                                             vmem_limit_bytes=VMEM_LIMIT),
        name="hgrn2_recurrence",
    )(q, f, v, zg, gn)


def kernel(x, p, w_in_even, conv_a_w, conv_b_w, a_log, dt_bias, gdn_norm_g, w_out_even, w_in_odd,
           lower_bounds, hgrn_norm_g, w_out_odd, ln_g, ln_b, w_pl, w_pl_gate):
    bsz, s, d = x.shape
    n = bsz * s
    assert d == D_MODEL and p.shape == (DEPTH, bsz, s, PL_DIM)
    assert s % max(IN_ROWS, GDN_ROWS, HGRN_ROWS) == 0 and n % OUT_ROWS == 0 and bsz % GDN_BATCH_ROWS == 0
    zb_end = 4 * CONV_A_WIDTH + 4 * GDN_WIDTH

    w_main = w_in_even[0, :, :zb_end].astype(BF16)
    wba = jnp.pad(w_in_even[0, :, zb_end:].astype(BF16), ((0, 0), (0, LANES - 2 * GDN_HEADS)))
    ya, q, k, v, zg, ba = _even_in_call(x, w_main, wba, conv_a_w[0], conv_b_w[0], tm=IN_ROWS)

    pad_lane = lambda a: jnp.pad(a, (GDN_HEADS, LANES - 2 * GDN_HEADS))
    arow = jnp.stack([pad_lane(a_log[0]), pad_lane(dt_bias[0])], axis=0)
    acol = jnp.stack([jnp.pad(a_log[0], (GDN_HEADS, 0)), jnp.pad(dt_bias[0], (GDN_HEADS, 0))], axis=1)
    og = _gdn_call(q, k, v, zg, ba, arow, acol, gdn_norm_g[0][None, :], tb=GDN_ROWS, bb=GDN_BATCH_ROWS)

    w_out = w_out_even[0].astype(BF16)
    p_rows = p.reshape(DEPTH, n, PL_DIM)
    x1 = _out_call([ya.reshape(n, -1), og.reshape(n, -1)], x.reshape(n, d), p_rows, 0,
                   [w_out[:CONV_A_WIDTH], w_out[CONV_A_WIDTH:]], w_pl_gate[0].astype(BF16),
                   w_pl[0].astype(BF16), ln_g[0][None, :], ln_b[0][None, :], tm=OUT_ROWS)

    hq, hf, hv, hzg = _odd_in_call(x1, w_in_odd[0].astype(BF16), lower_bounds, tm=IN_ROWS)
    shp = (bsz, s, HGRN_WIDTH)
    ho = _hgrn_call(hq.reshape(shp), hf.reshape(shp), hv.reshape(shp), hzg.reshape(shp),
                    hgrn_norm_g[0][None, :], tb=HGRN_ROWS, hb=HGRN_STEP_HEADS)
    out = _out_call([ho.reshape(n, -1)], x1, p_rows, 1, [w_out_odd[0].astype(BF16)],
                    w_pl_gate[1].astype(BF16), w_pl[1].astype(BF16), ln_g[1][None, :], ln_b[1][None, :],
                    tm=OUT_ROWS)
    return out.reshape(bsz, s, d)
```
